```python
import jax, jax.numpy as jnp
from jax import lax
import numpy as np

D_MODEL = 2048
BATCH = 8
SEQ = 2048
DEPTH = 1

D_MIX = D_MODEL
A_HEADS = 8
A_HEAD_DIM = 128
A_WIDTH = A_HEADS * A_HEAD_DIM
DILATED_PATTERNS = ((128, 1), (512, 4), (2048, 16))
B_HEADS = 8
B_V_DIM = 128
B_WIDTH = B_HEADS * B_V_DIM
B_NOPE_DIM = 128
B_ROPE_DIM = 64
Q_LORA_RANK = 512
KV_LORA_RANK = 256
ROPE_THETA = 10000.0
NORM_EPS = 1e-6
Q_BLOCK = 128
NEG_INF = -1e30
IN_SIZES = (A_WIDTH, A_WIDTH, A_WIDTH, A_WIDTH, Q_LORA_RANK, KV_LORA_RANK, B_ROPE_DIM, B_WIDTH)
D_IN = A_WIDTH * 4 + Q_LORA_RANK + KV_LORA_RANK + B_ROPE_DIM + B_WIDTH

kernel_name = 'hymba_dilated_mla_adaln_block'


def rms_norm(x, g):
    xf = x.astype(jnp.float32)
    y = xf * lax.rsqrt(jnp.mean(xf * xf, axis=-1, keepdims=True) + NORM_EPS)
    return (y * g.astype(jnp.float32)).astype(x.dtype)


def alibi_slopes(n_heads):
    return jnp.exp2(-8.0 * jnp.arange(1, n_heads + 1, dtype=jnp.float32) / n_heads)


def banded_attention(q, k, v, radius, dist_slope):
    *lead, L, hd = q.shape
    nlead = len(lead)
    blk = radius
    nb = -(-L // blk)
    lp = nb * blk
    qb = jnp.pad(q, [(0, 0)] * nlead + [(0, lp - L), (0, 0)]).reshape(*lead, nb, blk, hd)

    def windows(a):
        ap = jnp.pad(a, [(0, 0)] * nlead + [(blk, lp - L + blk), (0, 0)]).reshape(*lead, nb + 2, blk, hd)
        return jnp.concatenate([ap[..., :-2, :, :], ap[..., 1:-1, :, :], ap[..., 2:, :, :]], axis=-2)

    kw, vw = windows(k), windows(v)
    s = jnp.einsum('...nqd,...nkd->...nqk', qb, kw).astype(jnp.float32) * (hd ** -0.5)
    qpos = jnp.arange(lp).reshape(nb, blk)
    kpos = qpos[:, :1] - blk + jnp.arange(3 * blk)[None, :]
    rel = kpos[:, None, :] - qpos[:, :, None]
    valid = (jnp.abs(rel) <= radius) & (kpos >= 0)[:, None, :] & (kpos < L)[:, None, :]
    s = jnp.where(valid, s - dist_slope * jnp.abs(rel).astype(jnp.float32), NEG_INF)
    m = jnp.max(s, axis=-1, keepdims=True)
    p = jnp.exp(s - m)
    l = jnp.sum(p, axis=-1, keepdims=True)
    o = jnp.einsum('...nqk,...nkd->...nqd', p, vw.astype(jnp.float32)) / l
    lse = (m + jnp.log(l))[..., 0]
    o = o.reshape(*lead, lp, hd)[..., :L, :]
    lse = lse.reshape(*lead, lp)[..., :L]
    return o, lse


def dilated_mixture(q, k, v):
    B, T, H, hd = q.shape
    slopes = alibi_slopes(H)
    outs, lses = [], []
    for window, dil in DILATED_PATTERNS:
        radius = window // 2 // dil
        L = T // dil

        def to_strided(a):
            return a.reshape(B, L, dil, H, hd).transpose(0, 2, 3, 1, 4)

        o, lse = banded_attention(to_strided(q), to_strided(k), to_strided(v), radius,
                                  (slopes * dil)[:, None, None, None])
        outs.append(o.transpose(0, 3, 1, 2, 4).reshape(B, T, H, hd))
        lses.append(lse.transpose(0, 3, 1, 2).reshape(B, T, H))
    w = jax.nn.softmax(jnp.stack(lses, axis=0), axis=0)
    out = jnp.einsum('pbth,pbthd->bthd', w, jnp.stack(outs, axis=0))
    return out.astype(q.dtype)


def apply_rope(x, cos, sin):
    x1, x2 = jnp.split(x, 2, axis=-1)
    return jnp.concatenate([x1 * cos - x2 * sin, x1 * sin + x2 * cos], axis=-1)


def dense_attention(q, k, v):
    B, T, H, dk = q.shape
    nq = T // Q_BLOCK
    qb = q.reshape(B, nq, Q_BLOCK, H, dk).transpose(1, 0, 2, 3, 4)
    scale = dk ** -0.5

    def one_block(qi):
        s = jnp.einsum('bqhd,bkhd->bhqk', qi, k).astype(jnp.float32) * scale
        p = jax.nn.softmax(s, axis=-1)
        return jnp.einsum('bhqk,bkhd->bqhd', p.astype(v.dtype), v)

    o = lax.map(one_block, qb)
    return o.transpose(1, 0, 2, 3, 4).reshape(B, T, H, v.shape[-1])


def mla(cq, ckv, kpe, g_q_lora, w_uq, g_kv_lora, w_ukv):
    B, T, _ = cq.shape
    q = (rms_norm(cq, g_q_lora) @ w_uq).reshape(B, T, B_HEADS, B_NOPE_DIM + B_ROPE_DIM)
    q_nope, q_pe = q[..., :B_NOPE_DIM], q[..., B_NOPE_DIM:]
    kv = (rms_norm(ckv, g_kv_lora) @ w_ukv).reshape(B, T, B_HEADS, B_NOPE_DIM + B_V_DIM)
    k_nope, v = kv[..., :B_NOPE_DIM], kv[..., B_NOPE_DIM:]
    half = B_ROPE_DIM // 2
    pos = jnp.arange(T, dtype=jnp.float32)
    inv_freq = jnp.power(ROPE_THETA, -jnp.arange(half, dtype=jnp.float32) / half)
    ang = pos[:, None] * inv_freq[None, :]
    cos, sin = jnp.cos(ang).astype(cq.dtype), jnp.sin(ang).astype(cq.dtype)
    q_pe = apply_rope(q_pe, cos[None, :, None, :], sin[None, :, None, :])
    k_pe = apply_rope(kpe, cos[None], sin[None])
    k_pe = jnp.broadcast_to(k_pe[:, :, None, :], (B, T, B_HEADS, B_ROPE_DIM))
    q_full = jnp.concatenate([q_nope, q_pe], axis=-1)
    k_full = jnp.concatenate([k_nope, k_pe], axis=-1)
    return dense_attention(q_full, k_full, v)


def hybrid_layer(x, c, w_ada, b_ada, g_pre, w_in, g_q_lora, w_uq, g_kv_lora, w_ukv, w_out, g_post):
    B, T, _ = x.shape
    mod = jax.nn.silu(c) @ w_ada + b_ada
    shift, scale, gate = jnp.split(mod, 3, axis=-1)
    h = rms_norm(x, g_pre) * (1.0 + scale[:, None, :]) + shift[:, None, :]
    proj = h @ w_in
    split_points = [int(s) for s in np.cumsum(IN_SIZES)[:-1]]
    a_q, a_k, a_v, a_z, b_cq, b_ckv, b_kpe, b_z = jnp.split(proj, split_points, axis=-1)
    head_shape = (B, T, A_HEADS, A_HEAD_DIM)
    y_a = dilated_mixture(a_q.reshape(head_shape), a_k.reshape(head_shape), a_v.reshape(head_shape))
    y_a = y_a.reshape(B, T, A_WIDTH) * jax.nn.silu(a_z)
    y_b = mla(b_cq, b_ckv, b_kpe, g_q_lora, w_uq, g_kv_lora, w_ukv).reshape(B, T, B_WIDTH) * jax.nn.silu(b_z)
    y = jnp.concatenate([y_a, y_b], axis=-1) @ w_out
    return x + gate[:, None, :] * rms_norm(y, g_post)


def setup_inputs(seed: int = 0) -> dict:
    key = jax.random.key(seed)
    ks = jax.random.split(key, 14)
    f32 = jnp.float32

    def nrm(k, shape, s):
        return jax.random.normal(k, shape, f32) * s

    return {
        'x': nrm(ks[0], (BATCH, SEQ, D_MODEL), 1.0),
        'c': nrm(ks[1], (BATCH, D_MODEL), 1.0),
        'w_ada': nrm(ks[2], (DEPTH, D_MODEL, 3 * D_MODEL), 0.5 * D_MODEL ** -0.5),
        'b_ada': nrm(ks[3], (DEPTH, 3 * D_MODEL), 0.01),
        'g_pre': 1.0 + nrm(ks[4], (DEPTH, D_MODEL), 0.01),
        'w_in': nrm(ks[5], (DEPTH, D_MODEL, D_IN), D_MODEL ** -0.5),
        'g_q_lora': 1.0 + nrm(ks[6], (DEPTH, Q_LORA_RANK), 0.01),
        'w_uq': nrm(ks[7], (DEPTH, Q_LORA_RANK, B_HEADS * (B_NOPE_DIM + B_ROPE_DIM)), Q_LORA_RANK ** -0.5),
        'g_kv_lora': 1.0 + nrm(ks[8], (DEPTH, KV_LORA_RANK), 0.01),
        'w_ukv': nrm(ks[9], (DEPTH, KV_LORA_RANK, B_HEADS * (B_NOPE_DIM + B_V_DIM)), KV_LORA_RANK ** -0.5),
        'w_out': nrm(ks[10], (DEPTH, D_MIX, D_MODEL), D_MIX ** -0.5),
        'g_post': 1.0 + nrm(ks[11], (DEPTH, D_MODEL), 0.01),
    }


def reference(x, c, w_ada, b_ada, g_pre, w_in, g_q_lora, w_uq, g_kv_lora, w_ukv, w_out, g_post):
    for layer in range(DEPTH):
        x = hybrid_layer(x, c, w_ada[layer], b_ada[layer], g_pre[layer], w_in[layer],
                         g_q_lora[layer], w_uq[layer], g_kv_lora[layer], w_ukv[layer],
                         w_out[layer], g_post[layer])
    return x
```

```python
import functools

import jax
import jax.numpy as jnp
import numpy as np
from jax import lax
from jax.experimental import pallas as pl
from jax.experimental.pallas import tpu as pltpu

F32 = jnp.float32
BF16 = jnp.bfloat16

D_MODEL = 2048
A_HEADS = 8
A_HEAD_DIM = 128
A_WIDTH = A_HEADS * A_HEAD_DIM
DILATED_PATTERNS = ((128, 1), (512, 4), (2048, 16))
B_HEADS = 8
B_V_DIM = 128
B_WIDTH = B_HEADS * B_V_DIM
B_NOPE_DIM = 128
B_ROPE_DIM = 64
Q_LORA_RANK = 512
KV_LORA_RANK = 256
ROPE_THETA = 10000.0
NORM_EPS = 1e-6
NEG_INF = -1e30

LANES = 128
B_QK_PAD = 256
KPE_PAD = LANES

OFF_AQ = 0
OFF_AK = OFF_AQ + A_WIDTH
OFF_AV = OFF_AK + A_WIDTH
OFF_AZ = OFF_AV + A_WIDTH
OFF_CQ = OFF_AZ + A_WIDTH
OFF_CKV = OFF_CQ + Q_LORA_RANK
OFF_BZ = OFF_CKV + KV_LORA_RANK
OFF_KPE = OFF_BZ + B_WIDTH
D_IN_PAD = OFF_KPE + KPE_PAD

ROW_TILE = 512
BAND_Q = 128
MLA_Q = 256
VMEM_LIMIT = 56 * 1024 * 1024


def _silu(v):
    return v * (1.0 / (1.0 + jnp.exp(-v)))


def _rms(v, g):
    return v * lax.rsqrt(jnp.mean(v * v, axis=-1, keepdims=True) + NORM_EPS) * g


def _dot_nt(a, b):
    return lax.dot_general(a, b, (((1,), (1,)), ((), ())), preferred_element_type=F32)


def _ada_kernel(c_ref, w_ref, b_ref, o_ref):
    s = _silu(c_ref[...]).astype(BF16)
    o_ref[...] = jnp.dot(s, w_ref[...].astype(BF16), preferred_element_type=F32) + b_ref[...]


def _ada(c, w_ada, b_ada):
    bsz, d = c.shape
    n = w_ada.shape[1]
    tn = 512
    return pl.pallas_call(
        _ada_kernel,
        grid=(n // tn,),
        in_specs=[
            pl.BlockSpec((bsz, d), lambda j: (0, 0)),
            pl.BlockSpec((d, tn), lambda j: (0, j)),
            pl.BlockSpec((1, tn), lambda j: (0, j)),
        ],
        out_specs=pl.BlockSpec((bsz, tn), lambda j: (0, j)),
        out_shape=jax.ShapeDtypeStruct((bsz, n), F32),
        compiler_params=pltpu.CompilerParams(dimension_semantics=("arbitrary",)),
        name="ada_mod",
    )(c, w_ada, b_ada.reshape(1, n))


def _qkv_kernel(x_ref, mod_ref, g_ref, w_ref, h_ref, s1_ref, s4_ref, s16_ref, acc_ref):
    tm = x_ref.shape[0]
    shift = mod_ref[0, 0:1, :]
    scale = mod_ref[0, 1:2, :]
    h_ref[...] = (_rms(x_ref[...], g_ref[...]) * (1.0 + scale) + shift).astype(BF16)

    heads_per = acc_ref.shape[0]
    chunk = heads_per * A_HEAD_DIM
    q_scale = A_HEAD_DIM ** -0.5
    strided = ((s4_ref, DILATED_PATTERNS[1][1]), (s16_ref, DILATED_PATTERNS[2][1]))
    for t in range(3):
        for c0 in range(0, A_WIDTH, chunk):
            acc = jnp.dot(h_ref[...], w_ref[:, t * A_WIDTH + c0:t * A_WIDTH + c0 + chunk],
                          preferred_element_type=F32)
            if t == 0:
                acc = acc * q_scale
            for k in range(heads_per):
                head = c0 // A_HEAD_DIM + k
                a = acc[:, k * A_HEAD_DIM:(k + 1) * A_HEAD_DIM]
                s1_ref[t, 0, head, :, :] = a.astype(BF16)
                acc_ref[k, :, :] = a
            for k in range(heads_per):
                head = c0 // A_HEAD_DIM + k
                for ref, dil in strided:
                    for r in range(dil):
                        ref[t, 0, head, :, r * LANES:(r + 1) * LANES] = (
                            acc_ref[k, pl.ds(r, tm // dil, stride=dil), :].astype(BF16))


def _qkv_proj(x2, mod3, g_pre, w_qkv, bsz, seq):
    n_tok, d = x2.shape
    tm = ROW_TILE
    per_b = seq // tm
    hd = A_HEAD_DIM
    d4, d16 = DILATED_PATTERNS[1][1], DILATED_PATTERNS[2][1]
    head_major = lambda i: (0, i // per_b, 0, i % per_b, 0)
    return pl.pallas_call(
        _qkv_kernel,
        grid=(n_tok // tm,),
        in_specs=[
            pl.BlockSpec((tm, d), lambda i: (i, 0)),
            pl.BlockSpec((1, 3, d), lambda i: (i // per_b, 0, 0)),
            pl.BlockSpec((1, d), lambda i: (0, 0)),
            pl.BlockSpec(w_qkv.shape, lambda i: (0, 0), pipeline_mode=pl.Buffered(1)),
        ],
        out_specs=[
            pl.BlockSpec((tm, d), lambda i: (i, 0)),
            pl.BlockSpec((3, 1, A_HEADS, tm, hd), head_major),
            pl.BlockSpec((3, 1, A_HEADS, tm // d4, d4 * hd), head_major),
            pl.BlockSpec((3, 1, A_HEADS, tm // d16, d16 * hd), head_major),
        ],
        out_shape=[
            jax.ShapeDtypeStruct((n_tok, d), BF16),
            jax.ShapeDtypeStruct((3, bsz, A_HEADS, seq, hd), BF16),
            jax.ShapeDtypeStruct((3, bsz, A_HEADS, seq // d4, d4 * hd), BF16),
            jax.ShapeDtypeStruct((3, bsz, A_HEADS, seq // d16, d16 * hd), BF16),
        ],
        scratch_shapes=[pltpu.VMEM((4, tm, hd), F32)],
        compiler_params=pltpu.CompilerParams(dimension_semantics=("arbitrary",),
                                             vmem_limit_bytes=VMEM_LIMIT),
        name="qkv_proj",
    )(x2, mod3, g_pre.reshape(1, d), w_qkv)


def _rest_kernel(h_ref, w_ref, zg_ref, cq_ref, ckv_ref, bzg_ref, kpe_ref):
    def proj(off, width):
        return jnp.dot(h_ref[...], w_ref[:, off - OFF_AZ:off - OFF_AZ + width],
                       preferred_element_type=F32)

    chunk = 4 * LANES
    for c0 in range(0, A_WIDTH, chunk):
        zg_ref[:, c0:c0 + chunk] = _silu(proj(OFF_AZ + c0, chunk)).astype(BF16)
    cq_ref[...] = proj(OFF_CQ, Q_LORA_RANK).astype(BF16)
    ckv_ref[...] = proj(OFF_CKV, KV_LORA_RANK).astype(BF16)
    for c0 in range(0, B_WIDTH, chunk):
        bzg_ref[:, c0:c0 + chunk] = _silu(proj(OFF_BZ + c0, chunk)).astype(BF16)
    kpe_ref[...] = proj(OFF_KPE, KPE_PAD).astype(BF16)


def _rest_proj(h, w_rest):
    n_tok, d = h.shape
    tm = ROW_TILE
    row = lambda i: (i, 0)
    widths = (A_WIDTH, Q_LORA_RANK, KV_LORA_RANK, B_WIDTH, KPE_PAD)
    return pl.pallas_call(
        _rest_kernel,
        grid=(n_tok // tm,),
        in_specs=[
            pl.BlockSpec((tm, d), row),
            pl.BlockSpec(w_rest.shape, lambda i: (0, 0), pipeline_mode=pl.Buffered(1)),
        ],
        out_specs=[pl.BlockSpec((tm, w), row) for w in widths],
        out_shape=[jax.ShapeDtypeStruct((n_tok, w), BF16) for w in widths],
        compiler_params=pltpu.CompilerParams(dimension_semantics=("arbitrary",),
                                             vmem_limit_bytes=VMEM_LIMIT),
        name="rest_proj",
    )(h, w_rest)


def _band_bias(rows, cols, k_minus_q, radius, pen):
    rel = (lax.broadcasted_iota(jnp.int32, (rows, cols), 1) + k_minus_q
           - lax.broadcasted_iota(jnp.int32, (rows, cols), 0))
    dist = jnp.abs(rel)
    return jnp.where(dist <= radius, -pen * dist.astype(F32), NEG_INF)


def _band_block(q, k_win, v_win, bias):
    s = _dot_nt(q, k_win) + bias
    m = jnp.max(s, axis=-1, keepdims=True)
    p = jnp.exp(s - m).astype(BF16)
    v_ext = jnp.concatenate([v_win, jnp.ones_like(v_win)], axis=1)
    ol = jnp.dot(p, v_ext, preferred_element_type=F32)
    l = ol[:, A_HEAD_DIM:]
    o = ol[:, :A_HEAD_DIM] / l
    return o, m + jnp.log(l)


def _dil_kernel(slope_ref, s1_ref, s4_ref, s16_ref, zg_ref, y_ref, o_ref, lse_ref):
    seq = s1_ref.shape[3]
    slope = slope_ref[pl.program_id(1)]
    bq = BAND_Q
    win = 2 * bq
    views = (s1_ref, s4_ref, s16_ref)

    for p_idx, (window, dil) in enumerate(DILATED_PATTERNS):
        radius = window // 2 // dil
        length = seq // dil
        ref = views[p_idx]
        pen = slope * float(dil)
        n_blk = length // bq

        def store(res, lo, r, p_idx=p_idx, dil=dil):
            o, lse = res
            if dil == 1:
                rows = pl.ds(lo, bq)
            else:
                rows = pl.ds(lo * dil + r, bq, stride=dil)
            o_ref[p_idx, rows, :] = o
            lse_ref[p_idx, rows, :] = lse

        if n_blk == 1:
            bias = _band_bias(bq, bq, 0, radius, pen)
            for r in range(dil):
                cols = slice(r * LANES, (r + 1) * LANES)
                store(_band_block(ref[0, 0, 0, :, cols], ref[1, 0, 0, :, cols],
                                  ref[2, 0, 0, :, cols], bias), 0, r)
            continue

        bias_first = _band_bias(bq, win, 0, radius, pen)
        bias_mid = _band_bias(bq, win, -radius, radius, pen)
        bias_last = _band_bias(bq, win, -bq, radius, pen)
        for r in range(dil):
            cols = slice(r * LANES, (r + 1) * LANES)

            def block(q_lo, k_lo, bias, ref=ref, cols=cols):
                return _band_block(ref[0, 0, 0, pl.ds(q_lo, bq), cols],
                                   ref[1, 0, 0, pl.ds(k_lo, win), cols],
                                   ref[2, 0, 0, pl.ds(k_lo, win), cols], bias)

            store(block(0, 0, bias_first), 0, r)
            store(block(length - bq, length - win, bias_last), length - bq, r)
            if n_blk - 2 <= 2:
                for qi in range(1, n_blk - 1):
                    store(block(qi * bq, qi * bq - radius, bias_mid), qi * bq, r)
            else:
                def body(qi, carry, r=r, block=block, store=store, bias_mid=bias_mid, radius=radius):
                    q_lo = pl.multiple_of(qi * bq, bq)
                    k_lo = pl.multiple_of(qi * bq - radius, radius)
                    store(block(q_lo, k_lo, bias_mid), q_lo, r)
                    return carry

                lax.fori_loop(1, n_blk - 1, body, 0)

    rows_per = 256
    for c0 in range(0, seq, rows_per):
        rs = slice(c0, c0 + rows_per)
        l0, l1, l2 = lse_ref[0, rs, :], lse_ref[1, rs, :], lse_ref[2, rs, :]
        m = jnp.maximum(jnp.maximum(l0, l1), l2)
        e0, e1, e2 = jnp.exp(l0 - m), jnp.exp(l1 - m), jnp.exp(l2 - m)
        mix = (e0 * o_ref[0, rs, :] + e1 * o_ref[1, rs, :] + e2 * o_ref[2, rs, :]) / (e0 + e1 + e2)
        y_ref[0, rs, :] = (mix * zg_ref[0, rs, :].astype(F32)).astype(BF16)


def _dilated(views, zg3, slopes):
    _, bsz, heads, seq, hd = views[0].shape
    specs = [pl.BlockSpec((3, 1, 1) + v.shape[3:], lambda b, h: (0, b, h, 0, 0)) for v in views]
    tok_spec = pl.BlockSpec((1, seq, hd), lambda b, h: (b, 0, h))
    return pl.pallas_call(
        _dil_kernel,
        grid=(bsz, heads),
        in_specs=[pl.BlockSpec(memory_space=pltpu.SMEM)] + specs + [tok_spec],
        out_specs=tok_spec,
        out_shape=jax.ShapeDtypeStruct((bsz, seq, heads * hd), BF16),
        scratch_shapes=[pltpu.VMEM((3, seq, hd), F32), pltpu.VMEM((3, seq, hd), F32)],
        compiler_params=pltpu.CompilerParams(dimension_semantics=("arbitrary", "arbitrary"),
                                             vmem_limit_bytes=VMEM_LIMIT),
        name="dilated_mix",
    )(slopes, *views, zg3)


def _rope(pe, cc, sa, sb):
    return pe * cc + pltpu.roll(pe, 96, 1) * sa + pltpu.roll(pe, 32, 1) * sb


def _mla_proj_kernel(cq_ref, ckv_ref, kpe_ref, gq_ref, gkv_ref, wq_ref, wkv_ref, cc_ref, sa_ref, sb_ref,
                     qf_ref, kf_ref, v_ref):
    cc, sa, sb = cc_ref[...], sa_ref[...], sb_ref[...]
    q_scale = (B_NOPE_DIM + B_ROPE_DIM) ** -0.5
    cqn = _rms(cq_ref[...].astype(F32), gq_ref[...]).astype(BF16)
    for h in range(B_HEADS):
        q = jnp.dot(cqn, wq_ref[:, h * B_QK_PAD:(h + 1) * B_QK_PAD], preferred_element_type=F32)
        qf_ref[0, h, :, :B_NOPE_DIM] = (q[:, :B_NOPE_DIM] * q_scale).astype(BF16)
        qf_ref[0, h, :, B_NOPE_DIM:] = (_rope(q[:, B_NOPE_DIM:], cc, sa, sb) * q_scale).astype(BF16)
    k_rot = _rope(kpe_ref[...].astype(F32), cc, sa, sb).astype(BF16)
    ckvn = _rms(ckv_ref[...].astype(F32), gkv_ref[...]).astype(BF16)
    kv_w = B_NOPE_DIM + B_V_DIM
    for h in range(B_HEADS):
        kv = jnp.dot(ckvn, wkv_ref[:, h * kv_w:(h + 1) * kv_w], preferred_element_type=F32)
        kf_ref[0, h, :, :B_NOPE_DIM] = kv[:, :B_NOPE_DIM].astype(BF16)
        kf_ref[0, h, :, B_NOPE_DIM:] = k_rot
        v_ref[0, h, :, :] = kv[:, B_NOPE_DIM:].astype(BF16)


def _mla_proj(cq, ckv, kpe, g_q, g_kv, wq_p, wkv, cc, sa, sb, bsz, seq):
    n_tok = cq.shape[0]
    tm = ROW_TILE
    per_b = seq // tm
    row = lambda i: (i, 0)
    full = lambda i: (0, 0)
    pos = lambda i: (i % per_b, 0)
    head_major = lambda i: (i // per_b, 0, i % per_b, 0)
    return pl.pallas_call(
        _mla_proj_kernel,
        grid=(n_tok // tm,),
        in_specs=[
            pl.BlockSpec((tm, Q_LORA_RANK), row),
            pl.BlockSpec((tm, KV_LORA_RANK), row),
            pl.BlockSpec((tm, KPE_PAD), row),
            pl.BlockSpec((1, Q_LORA_RANK), full),
            pl.BlockSpec((1, KV_LORA_RANK), full),
            pl.BlockSpec(wq_p.shape, full),
            pl.BlockSpec(wkv.shape, full),
            pl.BlockSpec((tm, LANES), pos),
            pl.BlockSpec((tm, LANES), pos),
            pl.BlockSpec((tm, LANES), pos),
        ],
        out_specs=[
            pl.BlockSpec((1, B_HEADS, tm, B_QK_PAD), head_major),
            pl.BlockSpec((1, B_HEADS, tm, B_QK_PAD), head_major),
            pl.BlockSpec((1, B_HEADS, tm, B_V_DIM), head_major),
        ],
        out_shape=[
            jax.ShapeDtypeStruct((bsz, B_HEADS, seq, B_QK_PAD), BF16),
            jax.ShapeDtypeStruct((bsz, B_HEADS, seq, B_QK_PAD), BF16),
            jax.ShapeDtypeStruct((bsz, B_HEADS, seq, B_V_DIM), BF16),
        ],
        compiler_params=pltpu.CompilerParams(dimension_semantics=("arbitrary",),
                                             vmem_limit_bytes=VMEM_LIMIT),
        name="mla_proj",
    )(cq, ckv, kpe, g_q.reshape(1, -1), g_kv.reshape(1, -1), wq_p, wkv, cc, sa, sb)


def _mla_attn_kernel(q_ref, k_ref, v_ref, zg_ref, y_ref, vext_ref):
    seq = k_ref.shape[2]
    vext_ref[:, :B_V_DIM] = v_ref[0, 0, :, :]
    vext_ref[:, B_V_DIM:] = jnp.ones((seq, B_V_DIM), BF16)

    def body(qi, carry):
        rows = pl.ds(pl.multiple_of(qi * MLA_Q, MLA_Q), MLA_Q)
        s = _dot_nt(q_ref[0, 0, rows, :], k_ref[0, 0, :, :])
        m = jnp.max(s, axis=-1, keepdims=True)
        p = jnp.exp(s - m).astype(BF16)
        ol = jnp.dot(p, vext_ref[...], preferred_element_type=F32)
        o = ol[:, :B_V_DIM] / ol[:, B_V_DIM:]
        y_ref[0, rows, :] = (o * zg_ref[0, rows, :].astype(F32)).astype(BF16)
        return carry

    lax.fori_loop(0, seq // MLA_Q, body, 0)


def _mla_attn(qf, kf, v, bzg3):
    bsz, heads, seq, _ = qf.shape
    head_blk = lambda b, h: (b, h, 0, 0)
    tok_spec = pl.BlockSpec((1, seq, B_V_DIM), lambda b, h: (b, 0, h))
    return pl.pallas_call(
        _mla_attn_kernel,
        grid=(bsz, heads),
        in_specs=[
            pl.BlockSpec((1, 1, seq, B_QK_PAD), head_blk),
            pl.BlockSpec((1, 1, seq, B_QK_PAD), head_blk),
            pl.BlockSpec((1, 1, seq, B_V_DIM), head_blk),
            tok_spec,
        ],
        out_specs=tok_spec,
        out_shape=jax.ShapeDtypeStruct((bsz, seq, heads * B_V_DIM), BF16),
        scratch_shapes=[pltpu.VMEM((seq, 2 * B_V_DIM), BF16)],
        compiler_params=pltpu.CompilerParams(dimension_semantics=("arbitrary", "arbitrary"),
                                             vmem_limit_bytes=VMEM_LIMIT),
        name="mla_attn",
    )(qf, kf, v, bzg3)


def _out_kernel(ya_ref, yb_ref, w_ref, x_ref, mod_ref, g_ref, o_ref):
    y = (jnp.dot(ya_ref[...], w_ref[:A_WIDTH, :], preferred_element_type=F32)
         + jnp.dot(yb_ref[...], w_ref[A_WIDTH:, :], preferred_element_type=F32))
    gate = mod_ref[0, 2:3, :]
    o_ref[...] = x_ref[...] + gate * _rms(y, g_ref[...])


def _out_proj(ya, yb, w_out, x2, mod3, g_post, seq):
    n_tok, d = x2.shape
    tm = ROW_TILE
    per_b = seq // tm
    row = lambda i: (i, 0)
    return pl.pallas_call(
        _out_kernel,
        grid=(n_tok // tm,),
        in_specs=[
            pl.BlockSpec((tm, A_WIDTH), row),
            pl.BlockSpec((tm, B_WIDTH), row),
            pl.BlockSpec(w_out.shape, lambda i: (0, 0), pipeline_mode=pl.Buffered(1)),
            pl.BlockSpec((tm, d), row),
            pl.BlockSpec((1, 3, d), lambda i: (i // per_b, 0, 0)),
            pl.BlockSpec((1, d), lambda i: (0, 0)),
        ],
        out_specs=pl.BlockSpec((tm, d), row),
        out_shape=jax.ShapeDtypeStruct((n_tok, d), F32),
        compiler_params=pltpu.CompilerParams(dimension_semantics=("arbitrary",),
                                             vmem_limit_bytes=VMEM_LIMIT),
        name="out_proj",
    )(ya, yb, w_out, x2, mod3, g_post.reshape(1, d))


def _permute_w_in(w_in):
    sizes = (A_WIDTH, A_WIDTH, A_WIDTH, A_WIDTH, Q_LORA_RANK, KV_LORA_RANK, B_ROPE_DIM, B_WIDTH)
    a_q, a_k, a_v, a_z, b_cq, b_ckv, b_kpe, b_z = jnp.split(w_in, np.cumsum(sizes)[:-1].tolist(), axis=1)
    pad = jnp.zeros((w_in.shape[0], KPE_PAD - B_ROPE_DIM), w_in.dtype)
    return jnp.concatenate([a_q, a_k, a_v, a_z, b_cq, b_ckv, b_z, b_kpe, pad], axis=1).astype(BF16)


def _pad_w_uq(w_uq):
    r = w_uq.shape[0]
    w = w_uq.reshape(r, B_HEADS, B_NOPE_DIM + B_ROPE_DIM)
    pad = jnp.zeros((r, B_HEADS, B_QK_PAD - B_NOPE_DIM - B_ROPE_DIM), w_uq.dtype)
    return jnp.concatenate([w, pad], axis=2).reshape(r, B_HEADS * B_QK_PAD).astype(BF16)


def _rope_tables(seq):
    half = B_ROPE_DIM // 2
    pos = jnp.arange(seq, dtype=F32)
    inv_freq = jnp.power(ROPE_THETA, -jnp.arange(half, dtype=F32) / half)
    ang = pos[:, None] * inv_freq[None, :]
    cos, sin = jnp.cos(ang), jnp.sin(ang)
    z = jnp.zeros_like(cos)
    cc = jnp.concatenate([cos, cos, z, z], axis=1)
    sa = jnp.concatenate([-sin, z, z, z], axis=1)
    sb = jnp.concatenate([z, sin, z, z], axis=1)
    return cc, sa, sb


def _layer(x, c, w_ada, b_ada, g_pre, w_in, g_q_lora, w_uq, g_kv_lora, w_ukv, w_out, g_post):
    bsz, seq, d = x.shape
    x2 = x.reshape(bsz * seq, d)
    mod3 = _ada(c, w_ada, b_ada).reshape(bsz, 3, d)
    w_in_p = _permute_w_in(w_in)
    h, s1, s4, s16 = _qkv_proj(x2, mod3, g_pre, w_in_p[:, :OFF_AZ], bsz, seq)
    zg, cq, ckv, bzg, kpe = _rest_proj(h, w_in_p[:, OFF_AZ:])
    slopes = jnp.exp2(-8.0 * jnp.arange(1, A_HEADS + 1, dtype=F32) / A_HEADS)
    ya = _dilated((s1, s4, s16), zg.reshape(bsz, seq, A_WIDTH), slopes)
    cc, sa, sb = _rope_tables(seq)
    qf, kf, v = _mla_proj(cq, ckv, kpe, g_q_lora, g_kv_lora, _pad_w_uq(w_uq), w_ukv.astype(BF16),
                          cc, sa, sb, bsz, seq)
    yb = _mla_attn(qf, kf, v, bzg.reshape(bsz, seq, B_WIDTH))
    out = _out_proj(ya.reshape(bsz * seq, A_WIDTH), yb.reshape(bsz * seq, B_WIDTH),
                    w_out.astype(BF16), x2, mod3, g_post, seq)
    return out.reshape(bsz, seq, d)


def kernel(x, c, w_ada, b_ada, g_pre, w_in, g_q_lora, w_uq, g_kv_lora, w_ukv, w_out, g_post):
    for layer in range(w_ada.shape[0]):
        x = _layer(x, c, w_ada[layer], b_ada[layer], g_pre[layer], w_in[layer], g_q_lora[layer],
                   w_uq[layer], g_kv_lora[layer], w_ukv[layer], w_out[layer], g_post[layer])
    return x
```

```python
import functools

import jax
import jax.numpy as jnp
import numpy as np
from jax import lax
from jax.experimental import pallas as pl
from jax.experimental.pallas import tpu as pltpu

F32 = jnp.float32
BF16 = jnp.bfloat16

D_MODEL = 2048
A_HEADS = 8
A_HEAD_DIM = 128
A_WIDTH = A_HEADS * A_HEAD_DIM
DILATED_PATTERNS = ((128, 1), (512, 4), (2048, 16))
B_HEADS = 8
B_V_DIM = 128
B_WIDTH = B_HEADS * B_V_DIM
B_NOPE_DIM = 128
B_ROPE_DIM = 64
Q_LORA_RANK = 512
KV_LORA_RANK = 256
ROPE_THETA = 10000.0
NORM_EPS = 1e-6
NEG_INF = -1e30
LOG2E = 1.4426950408889634

LANES = 128
B_QK_PAD = 256
KPE_PAD = LANES

OFF_AQ = 0
OFF_AK = OFF_AQ + A_WIDTH
OFF_AV = OFF_AK + A_WIDTH
OFF_AZ = OFF_AV + A_WIDTH
OFF_CQ = OFF_AZ + A_WIDTH
OFF_CKV = OFF_CQ + Q_LORA_RANK
OFF_BZ = OFF_CKV + KV_LORA_RANK
OFF_KPE = OFF_BZ + B_WIDTH
D_IN_PAD = OFF_KPE + KPE_PAD

ROW_TILE = 512
BAND_Q = 128
MLA_Q = 256
VMEM_LIMIT = 56 * 1024 * 1024


def _silu(v):
    return v * (1.0 / (1.0 + jnp.exp(-v)))


def _rms(v, g):
    return v * lax.rsqrt(jnp.mean(v * v, axis=-1, keepdims=True) + NORM_EPS) * g


def _dot_nt(a, b):
    return lax.dot_general(a, b, (((1,), (1,)), ((), ())), preferred_element_type=F32)


def _ada_kernel(c_ref, w_ref, b_ref, o_ref):
    s = _silu(c_ref[...]).astype(BF16)
    o_ref[...] = jnp.dot(s, w_ref[...].astype(BF16), preferred_element_type=F32) + b_ref[...]


def _ada(c, w_ada, b_ada):
    bsz, d = c.shape
    n = w_ada.shape[1]
    tn = 512
    return pl.pallas_call(
        _ada_kernel,
        grid=(n // tn,),
        in_specs=[
            pl.BlockSpec((bsz, d), lambda j: (0, 0)),
            pl.BlockSpec((d, tn), lambda j: (0, j)),
            pl.BlockSpec((1, tn), lambda j: (0, j)),
        ],
        out_specs=pl.BlockSpec((bsz, tn), lambda j: (0, j)),
        out_shape=jax.ShapeDtypeStruct((bsz, n), F32),
        compiler_params=pltpu.CompilerParams(dimension_semantics=("arbitrary",)),
        name="ada_mod",
    )(c, w_ada, b_ada.reshape(1, n))


def _qkv_kernel(x_ref, mod_ref, g_ref, w_ref, h_ref, s1_ref, s4_ref, s16_ref, acc_ref):
    tm = x_ref.shape[0]
    shift = mod_ref[0, 0:1, :]
    scale = mod_ref[0, 1:2, :]
    h_ref[...] = (_rms(x_ref[...], g_ref[...]) * (1.0 + scale) + shift).astype(BF16)

    heads_per = acc_ref.shape[0]
    chunk = heads_per * A_HEAD_DIM
    q_scale = A_HEAD_DIM ** -0.5 * LOG2E
    strided = ((s4_ref, DILATED_PATTERNS[1][1]), (s16_ref, DILATED_PATTERNS[2][1]))
    for t in range(3):
        for c0 in range(0, A_WIDTH, chunk):
            acc = jnp.dot(h_ref[...], w_ref[:, t * A_WIDTH + c0:t * A_WIDTH + c0 + chunk],
                          preferred_element_type=F32)
            if t == 0:
                acc = acc * q_scale
            for k in range(heads_per):
                head = c0 // A_HEAD_DIM + k
                a = acc[:, k * A_HEAD_DIM:(k + 1) * A_HEAD_DIM]
                s1_ref[t, 0, head, :, :] = a.astype(BF16)
                acc_ref[k, :, :] = a
            for k in range(heads_per):
                head = c0 // A_HEAD_DIM + k
                for ref, dil in strided:
                    for r in range(dil):
                        ref[t, 0, head, :, r * LANES:(r + 1) * LANES] = (
                            acc_ref[k, pl.ds(r, tm // dil, stride=dil), :].astype(BF16))


def _qkv_proj(x2, mod3, g_pre, w_qkv, bsz, seq):
    n_tok, d = x2.shape
    tm = ROW_TILE
    per_b = seq // tm
    hd = A_HEAD_DIM
    d4, d16 = DILATED_PATTERNS[1][1], DILATED_PATTERNS[2][1]
    head_major = lambda i: (0, i // per_b, 0, i % per_b, 0)
    return pl.pallas_call(
        _qkv_kernel,
        grid=(n_tok // tm,),
        in_specs=[
            pl.BlockSpec((tm, d), lambda i: (i, 0)),
            pl.BlockSpec((1, 3, d), lambda i: (i // per_b, 0, 0)),
            pl.BlockSpec((1, d), lambda i: (0, 0)),
            pl.BlockSpec(w_qkv.shape, lambda i: (0, 0), pipeline_mode=pl.Buffered(1)),
        ],
        out_specs=[
            pl.BlockSpec((tm, d), lambda i: (i, 0)),
            pl.BlockSpec((3, 1, A_HEADS, tm, hd), head_major),
            pl.BlockSpec((3, 1, A_HEADS, tm // d4, d4 * hd), head_major),
            pl.BlockSpec((3, 1, A_HEADS, tm // d16, d16 * hd), head_major),
        ],
        out_shape=[
            jax.ShapeDtypeStruct((n_tok, d), BF16),
            jax.ShapeDtypeStruct((3, bsz, A_HEADS, seq, hd), BF16),
            jax.ShapeDtypeStruct((3, bsz, A_HEADS, seq // d4, d4 * hd), BF16),
            jax.ShapeDtypeStruct((3, bsz, A_HEADS, seq // d16, d16 * hd), BF16),
        ],
        scratch_shapes=[pltpu.VMEM((4, tm, hd), F32)],
        compiler_params=pltpu.CompilerParams(dimension_semantics=("arbitrary",),
                                             vmem_limit_bytes=VMEM_LIMIT),
        name="qkv_proj",
    )(x2, mod3, g_pre.reshape(1, d), w_qkv)


def _rest_kernel(h_ref, w_ref, zg_ref, cq_ref, ckv_ref, bzg_ref, kpe_ref):
    def proj(off, width):
        return jnp.dot(h_ref[...], w_ref[:, off - OFF_AZ:off - OFF_AZ + width],
                       preferred_element_type=F32)

    chunk = 4 * LANES
    for c0 in range(0, A_WIDTH, chunk):
        zg_ref[:, c0:c0 + chunk] = _silu(proj(OFF_AZ + c0, chunk)).astype(BF16)
    cq_ref[...] = proj(OFF_CQ, Q_LORA_RANK).astype(BF16)
    ckv_ref[...] = proj(OFF_CKV, KV_LORA_RANK).astype(BF16)
    for c0 in range(0, B_WIDTH, chunk):
        bzg_ref[:, c0:c0 + chunk] = _silu(proj(OFF_BZ + c0, chunk)).astype(BF16)
    kpe_ref[...] = proj(OFF_KPE, KPE_PAD).astype(BF16)


def _rest_proj(h, w_rest):
    n_tok, d = h.shape
    tm = ROW_TILE
    row = lambda i: (i, 0)
    widths = (A_WIDTH, Q_LORA_RANK, KV_LORA_RANK, B_WIDTH, KPE_PAD)
    return pl.pallas_call(
        _rest_kernel,
        grid=(n_tok // tm,),
        in_specs=[
            pl.BlockSpec((tm, d), row),
            pl.BlockSpec(w_rest.shape, lambda i: (0, 0), pipeline_mode=pl.Buffered(1)),
        ],
        out_specs=[pl.BlockSpec((tm, w), row) for w in widths],
        out_shape=[jax.ShapeDtypeStruct((n_tok, w), BF16) for w in widths],
        compiler_params=pltpu.CompilerParams(dimension_semantics=("arbitrary",),
                                             vmem_limit_bytes=VMEM_LIMIT),
        name="rest_proj",
    )(h, w_rest)


def _band_bias(rows, cols, k_minus_q, radius, pen):
    rel = (lax.broadcasted_iota(jnp.int32, (rows, cols), 1) + k_minus_q
           - lax.broadcasted_iota(jnp.int32, (rows, cols), 0))
    dist = jnp.abs(rel)
    return jnp.where(dist <= radius, -pen * dist.astype(F32), NEG_INF)


def _band_block(q, k_win, v_win, bias):
    s = _dot_nt(q, k_win) + bias
    m = jnp.max(s, axis=-1, keepdims=True)
    p = jnp.exp2(s - m).astype(BF16)
    v_ext = jnp.concatenate([v_win, jnp.ones_like(v_win)], axis=1)
    ol = jnp.dot(p, v_ext, preferred_element_type=F32)
    l = ol[:, A_HEAD_DIM:]
    o = ol[:, :A_HEAD_DIM] / l
    return o, m + jnp.log2(l)


def _dil_kernel(slope_ref, s1_ref, s4_ref, s16_ref, zg_ref, y_ref, o_ref, lse_ref):
    seq = s1_ref.shape[3]
    slope = slope_ref[pl.program_id(1)]
    bq = BAND_Q
    win = 2 * bq
    views = (s1_ref, s4_ref, s16_ref)

    for p_idx, (window, dil) in enumerate(DILATED_PATTERNS):
        radius = window // 2 // dil
        length = seq // dil
        ref = views[p_idx]
        pen = slope * (float(dil) * LOG2E)
        n_blk = length // bq

        def store(res, lo, r, p_idx=p_idx, dil=dil):
            o, lse = res
            if dil == 1:
                rows = pl.ds(lo, bq)
            else:
                rows = pl.ds(lo * dil + r, bq, stride=dil)
            o_ref[p_idx, rows, :] = o
            lse_ref[p_idx, rows, :] = lse

        if n_blk == 1:
            bias = _band_bias(bq, bq, 0, radius, pen)
            for r in range(dil):
                cols = slice(r * LANES, (r + 1) * LANES)
                store(_band_block(ref[0, 0, 0, :, cols], ref[1, 0, 0, :, cols],
                                  ref[2, 0, 0, :, cols], bias), 0, r)
            continue

        bias_first = _band_bias(bq, win, 0, radius, pen)
        bias_mid = _band_bias(bq, win, -radius, radius, pen)
        bias_last = _band_bias(bq, win, -bq, radius, pen)
        for r in range(dil):
            cols = slice(r * LANES, (r + 1) * LANES)

            def block(q_lo, k_lo, bias, ref=ref, cols=cols):
                return _band_block(ref[0, 0, 0, pl.ds(q_lo, bq), cols],
                                   ref[1, 0, 0, pl.ds(k_lo, win), cols],
                                   ref[2, 0, 0, pl.ds(k_lo, win), cols], bias)

            store(block(0, 0, bias_first), 0, r)
            store(block(length - bq, length - win, bias_last), length - bq, r)
            for qi in range(1, n_blk - 1):
                store(block(qi * bq, qi * bq - radius, bias_mid), qi * bq, r)

    rows_per = 256
    for c0 in range(0, seq, rows_per):
        rs = slice(c0, c0 + rows_per)
        l0, l1, l2 = lse_ref[0, rs, :], lse_ref[1, rs, :], lse_ref[2, rs, :]
        m = jnp.maximum(jnp.maximum(l0, l1), l2)
        e0, e1, e2 = jnp.exp2(l0 - m), jnp.exp2(l1 - m), jnp.exp2(l2 - m)
        mix = (e0 * o_ref[0, rs, :] + e1 * o_ref[1, rs, :] + e2 * o_ref[2, rs, :]) / (e0 + e1 + e2)
        y_ref[0, rs, :] = (mix * zg_ref[0, rs, :].astype(F32)).astype(BF16)


def _dilated(views, zg3, slopes):
    _, bsz, heads, seq, hd = views[0].shape
    specs = [pl.BlockSpec((3, 1, 1) + v.shape[3:], lambda b, h: (0, b, h, 0, 0)) for v in views]
    tok_spec = pl.BlockSpec((1, seq, hd), lambda b, h: (b, 0, h))
    return pl.pallas_call(
        _dil_kernel,
        grid=(bsz, heads),
        in_specs=[pl.BlockSpec(memory_space=pltpu.SMEM)] + specs + [tok_spec],
        out_specs=tok_spec,
        out_shape=jax.ShapeDtypeStruct((bsz, seq, heads * hd), BF16),
        scratch_shapes=[pltpu.VMEM((3, seq, hd), F32), pltpu.VMEM((3, seq, hd), F32)],
        compiler_params=pltpu.CompilerParams(dimension_semantics=("arbitrary", "arbitrary"),
                                             vmem_limit_bytes=VMEM_LIMIT),
        name="dilated_mix",
    )(slopes, *views, zg3)


def _rope(pe, cc, sa, sb):
    return pe * cc + pltpu.roll(pe, 96, 1) * sa + pltpu.roll(pe, 32, 1) * sb


def _mla_proj_kernel(cq_ref, ckv_ref, kpe_ref, gq_ref, gkv_ref, wq_ref, wkv_ref, cc_ref, sa_ref, sb_ref,
                     qf_ref, kf_ref, v_ref):
    cc, sa, sb = cc_ref[...], sa_ref[...], sb_ref[...]
    q_scale = (B_NOPE_DIM + B_ROPE_DIM) ** -0.5 * LOG2E
    cqn = _rms(cq_ref[...].astype(F32), gq_ref[...]).astype(BF16)
    for h in range(B_HEADS):
        q = jnp.dot(cqn, wq_ref[:, h * B_QK_PAD:(h + 1) * B_QK_PAD], preferred_element_type=F32)
        qf_ref[0, h, :, :B_NOPE_DIM] = (q[:, :B_NOPE_DIM] * q_scale).astype(BF16)
        qf_ref[0, h, :, B_NOPE_DIM:] = (_rope(q[:, B_NOPE_DIM:], cc, sa, sb) * q_scale).astype(BF16)
    k_rot = _rope(kpe_ref[...].astype(F32), cc, sa, sb).astype(BF16)
    ckvn = _rms(ckv_ref[...].astype(F32), gkv_ref[...]).astype(BF16)
    kv_w = B_NOPE_DIM + B_V_DIM
    for h in range(B_HEADS):
        kv = jnp.dot(ckvn, wkv_ref[:, h * kv_w:(h + 1) * kv_w], preferred_element_type=F32)
        kf_ref[0, h, :, :B_NOPE_DIM] = kv[:, :B_NOPE_DIM].astype(BF16)
        kf_ref[0, h, :, B_NOPE_DIM:] = k_rot
        v_ref[0, h, :, :] = kv[:, B_NOPE_DIM:].astype(BF16)


def _mla_proj(cq, ckv, kpe, g_q, g_kv, wq_p, wkv, cc, sa, sb, bsz, seq):
    n_tok = cq.shape[0]
    tm = ROW_TILE
    per_b = seq // tm
    row = lambda i: (i, 0)
    full = lambda i: (0, 0)
    pos = lambda i: (i % per_b, 0)
    head_major = lambda i: (i // per_b, 0, i % per_b, 0)
    return pl.pallas_call(
        _mla_proj_kernel,
        grid=(n_tok // tm,),
        in_specs=[
            pl.BlockSpec((tm, Q_LORA_RANK), row),
            pl.BlockSpec((tm, KV_LORA_RANK), row),
            pl.BlockSpec((tm, KPE_PAD), row),
            pl.BlockSpec((1, Q_LORA_RANK), full),
            pl.BlockSpec((1, KV_LORA_RANK), full),
            pl.BlockSpec(wq_p.shape, full),
            pl.BlockSpec(wkv.shape, full),
            pl.BlockSpec((tm, LANES), pos),
            pl.BlockSpec((tm, LANES), pos),
            pl.BlockSpec((tm, LANES), pos),
        ],
        out_specs=[
            pl.BlockSpec((1, B_HEADS, tm, B_QK_PAD), head_major),
            pl.BlockSpec((1, B_HEADS, tm, B_QK_PAD), head_major),
            pl.BlockSpec((1, B_HEADS, tm, B_V_DIM), head_major),
        ],
        out_shape=[
            jax.ShapeDtypeStruct((bsz, B_HEADS, seq, B_QK_PAD), BF16),
            jax.ShapeDtypeStruct((bsz, B_HEADS, seq, B_QK_PAD), BF16),
            jax.ShapeDtypeStruct((bsz, B_HEADS, seq, B_V_DIM), BF16),
        ],
        compiler_params=pltpu.CompilerParams(dimension_semantics=("arbitrary",),
                                             vmem_limit_bytes=VMEM_LIMIT),
        name="mla_proj",
    )(cq, ckv, kpe, g_q.reshape(1, -1), g_kv.reshape(1, -1), wq_p, wkv, cc, sa, sb)


def _mla_attn_kernel(q_ref, k_ref, v_ref, zg_ref, y_ref, vext_ref, s_ref):
    seq = k_ref.shape[2]
    n_blk = seq // MLA_Q
    vext_ref[:, :B_V_DIM] = v_ref[0, 0, :, :]
    vext_ref[:, B_V_DIM:] = jnp.ones((seq, B_V_DIM), BF16)

    def rows_of(blk):
        start = blk * MLA_Q
        return pl.ds(start if isinstance(blk, int) else pl.multiple_of(start, MLA_Q), MLA_Q)

    def scores(blk, slot):
        s_ref[slot] = _dot_nt(q_ref[0, 0, rows_of(blk), :], k_ref[0, 0, :, :])

    def softmax_pv(blk, slot):
        s = s_ref[slot]
        m = jnp.max(s, axis=-1, keepdims=True)
        p = jnp.exp2(s - m).astype(BF16)
        ol = jnp.dot(p, vext_ref[...], preferred_element_type=F32)
        o = ol[:, :B_V_DIM] / ol[:, B_V_DIM:]
        rows = rows_of(blk)
        y_ref[0, rows, :] = (o * zg_ref[0, rows, :].astype(F32)).astype(BF16)

    scores(0, 0)

    def body(j, carry):
        blk = 2 * j
        scores(blk + 1, 1)
        softmax_pv(blk, 0)
        scores(blk + 2, 0)
        softmax_pv(blk + 1, 1)
        return carry

    lax.fori_loop(0, n_blk // 2 - 1, body, 0)
    scores(n_blk - 1, 1)
    softmax_pv(n_blk - 2, 0)
    softmax_pv(n_blk - 1, 1)


def _mla_attn(qf, kf, v, bzg3):
    bsz, heads, seq, _ = qf.shape
    head_blk = lambda b, h: (b, h, 0, 0)
    tok_spec = pl.BlockSpec((1, seq, B_V_DIM), lambda b, h: (b, 0, h))
    return pl.pallas_call(
        _mla_attn_kernel,
        grid=(bsz, heads),
        in_specs=[
            pl.BlockSpec((1, 1, seq, B_QK_PAD), head_blk),
            pl.BlockSpec((1, 1, seq, B_QK_PAD), head_blk),
            pl.BlockSpec((1, 1, seq, B_V_DIM), head_blk),
            tok_spec,
        ],
        out_specs=tok_spec,
        out_shape=jax.ShapeDtypeStruct((bsz, seq, heads * B_V_DIM), BF16),
        scratch_shapes=[pltpu.VMEM((seq, 2 * B_V_DIM), BF16), pltpu.VMEM((2, MLA_Q, seq), F32)],
        compiler_params=pltpu.CompilerParams(dimension_semantics=("arbitrary", "arbitrary"),
                                             vmem_limit_bytes=VMEM_LIMIT),
        name="mla_attn",
    )(qf, kf, v, bzg3)


def _out_kernel(ya_ref, yb_ref, w_ref, x_ref, mod_ref, g_ref, o_ref):
    y = (jnp.dot(ya_ref[...], w_ref[:A_WIDTH, :], preferred_element_type=F32)
         + jnp.dot(yb_ref[...], w_ref[A_WIDTH:, :], preferred_element_type=F32))
    gate = mod_ref[0, 2:3, :]
    o_ref[...] = x_ref[...] + gate * _rms(y, g_ref[...])


def _out_proj(ya, yb, w_out, x2, mod3, g_post, seq):
    n_tok, d = x2.shape
    tm = ROW_TILE
    per_b = seq // tm
    row = lambda i: (i, 0)
    return pl.pallas_call(
        _out_kernel,
        grid=(n_tok // tm,),
        in_specs=[
            pl.BlockSpec((tm, A_WIDTH), row),
            pl.BlockSpec((tm, B_WIDTH), row),
            pl.BlockSpec(w_out.shape, lambda i: (0, 0), pipeline_mode=pl.Buffered(1)),
            pl.BlockSpec((tm, d), row),
            pl.BlockSpec((1, 3, d), lambda i: (i // per_b, 0, 0)),
            pl.BlockSpec((1, d), lambda i: (0, 0)),
        ],
        out_specs=pl.BlockSpec((tm, d), row),
        out_shape=jax.ShapeDtypeStruct((n_tok, d), F32),
        compiler_params=pltpu.CompilerParams(dimension_semantics=("arbitrary",),
                                             vmem_limit_bytes=VMEM_LIMIT),
        name="out_proj",
    )(ya, yb, w_out, x2, mod3, g_post.reshape(1, d))


def _split_w_in(w_in):
    w_qkv = w_in[:, :OFF_AZ].astype(BF16)
    kpe0 = OFF_CKV + KV_LORA_RANK
    bz0 = kpe0 + B_ROPE_DIM
    pad = jnp.zeros((w_in.shape[0], KPE_PAD - B_ROPE_DIM), BF16)
    w_rest = jnp.concatenate([w_in[:, OFF_AZ:kpe0].astype(BF16), w_in[:, bz0:].astype(BF16),
                              w_in[:, kpe0:bz0].astype(BF16), pad], axis=1)
    return w_qkv, w_rest


def _pad_w_uq(w_uq):
    r = w_uq.shape[0]
    w = w_uq.reshape(r, B_HEADS, B_NOPE_DIM + B_ROPE_DIM)
    pad = jnp.zeros((r, B_HEADS, B_QK_PAD - B_NOPE_DIM - B_ROPE_DIM), w_uq.dtype)
    return jnp.concatenate([w, pad], axis=2).reshape(r, B_HEADS * B_QK_PAD).astype(BF16)


def _rope_tables(seq):
    half = B_ROPE_DIM // 2
    pos = jnp.arange(seq, dtype=F32)
    inv_freq = jnp.power(ROPE_THETA, -jnp.arange(half, dtype=F32) / half)
    ang = pos[:, None] * inv_freq[None, :]
    cos, sin = jnp.cos(ang), jnp.sin(ang)
    z = jnp.zeros_like(cos)
    cc = jnp.concatenate([cos, cos, z, z], axis=1)
    sa = jnp.concatenate([-sin, z, z, z], axis=1)
    sb = jnp.concatenate([z, sin, z, z], axis=1)
    return cc, sa, sb


def _layer(x, c, w_ada, b_ada, g_pre, w_in, g_q_lora, w_uq, g_kv_lora, w_ukv, w_out, g_post):
    bsz, seq, d = x.shape
    x2 = x.reshape(bsz * seq, d)
    mod3 = _ada(c, w_ada, b_ada).reshape(bsz, 3, d)
    w_qkv, w_rest = _split_w_in(w_in)
    h, s1, s4, s16 = _qkv_proj(x2, mod3, g_pre, w_qkv, bsz, seq)
    zg, cq, ckv, bzg, kpe = _rest_proj(h, w_rest)
    slopes = jnp.exp2(-8.0 * jnp.arange(1, A_HEADS + 1, dtype=F32) / A_HEADS)
    ya = _dilated((s1, s4, s16), zg.reshape(bsz, seq, A_WIDTH), slopes)
    cc, sa, sb = _rope_tables(seq)
    qf, kf, v = _mla_proj(cq, ckv, kpe, g_q_lora, g_kv_lora, _pad_w_uq(w_uq), w_ukv.astype(BF16),
                          cc, sa, sb, bsz, seq)
    yb = _mla_attn(qf, kf, v, bzg.reshape(bsz, seq, B_WIDTH))
    out = _out_proj(ya.reshape(bsz * seq, A_WIDTH), yb.reshape(bsz * seq, B_WIDTH),
                    w_out.astype(BF16), x2, mod3, g_post, seq)
    return out.reshape(bsz, seq, d)


def kernel(x, c, w_ada, b_ada, g_pre, w_in, g_q_lora, w_uq, g_kv_lora, w_ukv, w_out, g_post):
    for layer in range(w_ada.shape[0]):
        x = _layer(x, c, w_ada[layer], b_ada[layer], g_pre[layer], w_in[layer], g_q_lora[layer],
                   w_uq[layer], g_kv_lora[layer], w_ukv[layer], w_out[layer], g_post[layer])
    return x
```

```python
import jax
import jax.numpy as jnp
from jax import lax
from jax.experimental import pallas as pl
from jax.experimental.pallas import tpu as pltpu

F32 = jnp.float32
BF16 = jnp.bfloat16

D_MODEL = 2048
A_HEADS = 8
A_HEAD_DIM = 128
A_WIDTH = A_HEADS * A_HEAD_DIM
DILATED_PATTERNS = ((128, 1), (512, 4), (2048, 16))
B_HEADS = 8
B_V_DIM = 128
B_WIDTH = B_HEADS * B_V_DIM
B_NOPE_DIM = 128
B_ROPE_DIM = 64
Q_LORA_RANK = 512
KV_LORA_RANK = 256
ROPE_THETA = 10000.0
NORM_EPS = 1e-6
NEG_INF = -1e30
LOG2E = 1.4426950408889634

LANES = 128
B_QK_PAD = 256
KPE_PAD = LANES

OFF_AZ = 3 * A_WIDTH
OFF_CQ = OFF_AZ + A_WIDTH
OFF_CKV = OFF_CQ + Q_LORA_RANK

ROW_TILE = 512
BAND_Q = 128
MLA_Q = 256
VMEM_LIMIT = 56 * 1024 * 1024


def _silu(v):
    return v * (1.0 / (1.0 + jnp.exp(-v)))


def _rms(v, g):
    return v * lax.rsqrt(jnp.mean(v * v, axis=-1, keepdims=True) + NORM_EPS) * g


def _dot_nt(a, b):
    return lax.dot_general(a, b, (((1,), (1,)), ((), ())), preferred_element_type=F32)


def _ada_kernel(c_ref, w_ref, b_ref, o_ref):
    s = _silu(c_ref[...]).astype(BF16)
    o_ref[...] = jnp.dot(s, w_ref[...].astype(BF16), preferred_element_type=F32) + b_ref[...]


def _ada(c, w_ada, b_ada):
    bsz, d = c.shape
    n = w_ada.shape[1]
    tn = 512
    return pl.pallas_call(
        _ada_kernel,
        grid=(n // tn,),
        in_specs=[
            pl.BlockSpec((bsz, d), lambda j: (0, 0)),
            pl.BlockSpec((d, tn), lambda j: (0, j)),
            pl.BlockSpec((1, tn), lambda j: (0, j)),
        ],
        out_specs=pl.BlockSpec((bsz, tn), lambda j: (0, j)),
        out_shape=jax.ShapeDtypeStruct((bsz, n), F32),
        compiler_params=pltpu.CompilerParams(dimension_semantics=("arbitrary",)),
        name="ada_mod",
    )(c, w_ada, b_ada.reshape(1, n))


def _cast_w_in_kernel(w_ref, qkv_ref, mid_ref, bz_ref, kpe_ref):
    kpe0 = OFF_CKV + KV_LORA_RANK
    qkv_ref[...] = w_ref[:, :OFF_AZ].astype(BF16)
    mid_ref[...] = w_ref[:, OFF_AZ:kpe0].astype(BF16)
    bz_ref[...] = w_ref[:, kpe0 + B_ROPE_DIM:].astype(BF16)
    tile = w_ref[:, kpe0:kpe0 + KPE_PAD]
    lane = lax.broadcasted_iota(jnp.int32, tile.shape, 1)
    kpe_ref[...] = jnp.where(lane < B_ROPE_DIM, tile, 0.0).astype(BF16)


def _cast_w_in(w_in):
    d, n = w_in.shape
    tr = 256
    widths = (OFF_AZ, OFF_CKV + KV_LORA_RANK - OFF_AZ, B_WIDTH, KPE_PAD)
    return pl.pallas_call(
        _cast_w_in_kernel,
        grid=(d // tr,),
        in_specs=[pl.BlockSpec((tr, n), lambda i: (i, 0))],
        out_specs=[pl.BlockSpec((tr, w), lambda i: (i, 0)) for w in widths],
        out_shape=[jax.ShapeDtypeStruct((d, w), BF16) for w in widths],
        compiler_params=pltpu.CompilerParams(dimension_semantics=("arbitrary",),
                                             vmem_limit_bytes=VMEM_LIMIT),
        name="cast_w_in",
    )(w_in)


def _qkv_kernel(x_ref, mod_ref, g_ref, w_ref, h_ref, s1_ref, s4_ref, s16_ref, acc_ref, acc4_ref):
    tm = x_ref.shape[0]
    d4 = DILATED_PATTERNS[1][1]
    x = x_ref[...]
    gain = g_ref[...] * (1.0 + mod_ref[0, 1:2, :])
    inv = lax.rsqrt(jnp.mean(x * x, axis=-1, keepdims=True) + NORM_EPS)
    h_ref[...] = (x * inv * gain + mod_ref[0, 0:1, :]).astype(BF16)

    heads_per = acc_ref.shape[0]
    chunk = heads_per * A_HEAD_DIM
    q_scale = A_HEAD_DIM ** -0.5 * LOG2E
    for t in range(3):
        for c0 in range(0, A_WIDTH, chunk):
            acc = jnp.dot(h_ref[...], w_ref[:, t * A_WIDTH + c0:t * A_WIDTH + c0 + chunk],
                          preferred_element_type=F32)
            if t == 0:
                acc = acc * q_scale
            for k in range(heads_per):
                head = c0 // A_HEAD_DIM + k
                a = acc[:, k * A_HEAD_DIM:(k + 1) * A_HEAD_DIM]
                s1_ref[t, 0, head, :, :] = a.astype(BF16)
                acc_ref[k, :, :] = a
            for k in range(heads_per):
                head = c0 // A_HEAD_DIM + k
                for r4 in range(d4):
                    g4 = acc_ref[k, pl.ds(r4, tm // d4, stride=d4), :]
                    s4_ref[t, 0, head, :, r4 * LANES:(r4 + 1) * LANES] = g4.astype(BF16)
                    acc4_ref[k, r4, :, :] = g4
                for r4 in range(d4):
                    for a4 in range(d4):
                        r = r4 + d4 * a4
                        s16_ref[t, 0, head, :, r * LANES:(r + 1) * LANES] = (
                            acc4_ref[k, r4, pl.ds(a4, tm // (d4 * d4), stride=d4), :].astype(BF16))


def _qkv_proj(x2, mod3, g_pre, w_qkv, bsz, seq):
    n_tok, d = x2.shape
    tm = ROW_TILE
    per_b = seq // tm
    hd = A_HEAD_DIM
    d4, d16 = DILATED_PATTERNS[1][1], DILATED_PATTERNS[2][1]
    head_major = lambda i: (0, i // per_b, 0, i % per_b, 0)
    return pl.pallas_call(
        _qkv_kernel,
        grid=(n_tok // tm,),
        in_specs=[
            pl.BlockSpec((tm, d), lambda i: (i, 0)),
            pl.BlockSpec((1, 3, d), lambda i: (i // per_b, 0, 0)),
            pl.BlockSpec((1, d), lambda i: (0, 0)),
            pl.BlockSpec(w_qkv.shape, lambda i: (0, 0), pipeline_mode=pl.Buffered(1)),
        ],
        out_specs=[
            pl.BlockSpec((tm, d), lambda i: (i, 0)),
            pl.BlockSpec((3, 1, A_HEADS, tm, hd), head_major),
            pl.BlockSpec((3, 1, A_HEADS, tm // d4, d4 * hd), head_major),
            pl.BlockSpec((3, 1, A_HEADS, tm // d16, d16 * hd), head_major),
        ],
        out_shape=[
            jax.ShapeDtypeStruct((n_tok, d), BF16),
            jax.ShapeDtypeStruct((3, bsz, A_HEADS, seq, hd), BF16),
            jax.ShapeDtypeStruct((3, bsz, A_HEADS, seq // d4, d4 * hd), BF16),
            jax.ShapeDtypeStruct((3, bsz, A_HEADS, seq // d16, d16 * hd), BF16),
        ],
        scratch_shapes=[pltpu.VMEM((4, tm, hd), F32), pltpu.VMEM((4, d4, tm // d4, hd), F32)],
        compiler_params=pltpu.CompilerParams(dimension_semantics=("arbitrary",),
                                             vmem_limit_bytes=VMEM_LIMIT),
        name="qkv_proj",
    )(x2, mod3, g_pre.reshape(1, d), w_qkv)


def _band_bias(rows, cols, k_minus_q, radius, pen):
    rel = (lax.broadcasted_iota(jnp.int32, (rows, cols), 1) + k_minus_q
           - lax.broadcasted_iota(jnp.int32, (rows, cols), 0))
    dist = jnp.abs(rel)
    return jnp.where(dist <= radius, -pen * dist.astype(F32), NEG_INF)


def _band_block(q, k_win, v_win, bias):
    s = _dot_nt(q, k_win) + bias
    m = jnp.max(s, axis=-1, keepdims=True)
    p = jnp.exp2(s - m).astype(BF16)
    v_ext = jnp.concatenate([v_win, jnp.ones_like(v_win)], axis=1)
    ol = jnp.dot(p, v_ext, preferred_element_type=F32)
    l = ol[:, A_HEAD_DIM:]
    o = ol[:, :A_HEAD_DIM] / l
    return o, m + jnp.log2(l)


def _dil_kernel(slope_ref, s1_ref, s4_ref, s16_ref, zg_ref, y_ref, o_ref, lse_ref):
    seq = s1_ref.shape[3]
    slope = slope_ref[pl.program_id(1)]
    bq = BAND_Q
    win = 2 * bq
    views = (s1_ref, s4_ref, s16_ref)

    for p_idx, (window, dil) in enumerate(DILATED_PATTERNS):
        radius = window // 2 // dil
        length = seq // dil
        ref = views[p_idx]
        pen = slope * (float(dil) * LOG2E)
        n_blk = length // bq

        def store(res, lo, r, p_idx=p_idx, dil=dil):
            o, lse = res
            if dil == 1:
                rows = pl.ds(lo, bq)
            else:
                rows = pl.ds(lo * dil + r, bq, stride=dil)
            o_ref[p_idx, rows, :] = o
            lse_ref[p_idx, rows, :] = lse

        if n_blk == 1:
            bias = _band_bias(bq, bq, 0, radius, pen)
            for r in range(dil):
                cols = slice(r * LANES, (r + 1) * LANES)
                store(_band_block(ref[0, 0, 0, :, cols], ref[1, 0, 0, :, cols],
                                  ref[2, 0, 0, :, cols], bias), 0, r)
            continue

        bias_first = _band_bias(bq, win, 0, radius, pen)
        bias_mid = _band_bias(bq, win, -radius, radius, pen)
        bias_last = _band_bias(bq, win, -bq, radius, pen)
        for r in range(dil):
            cols = slice(r * LANES, (r + 1) * LANES)

            def block(q_lo, k_lo, bias, ref=ref, cols=cols):
                return _band_block(ref[0, 0, 0, pl.ds(q_lo, bq), cols],
                                   ref[1, 0, 0, pl.ds(k_lo, win), cols],
                                   ref[2, 0, 0, pl.ds(k_lo, win), cols], bias)

            store(block(0, 0, bias_first), 0, r)
            store(block(length - bq, length - win, bias_last), length - bq, r)
            for qi in range(1, n_blk - 1):
                store(block(qi * bq, qi * bq - radius, bias_mid), qi * bq, r)

    rows_per = 256
    for c0 in range(0, seq, rows_per):
        rs = slice(c0, c0 + rows_per)
        l0, l1, l2 = lse_ref[0, rs, :], lse_ref[1, rs, :], lse_ref[2, rs, :]
        m = jnp.maximum(jnp.maximum(l0, l1), l2)
        e0, e1, e2 = jnp.exp2(l0 - m), jnp.exp2(l1 - m), jnp.exp2(l2 - m)
        mix = (e0 * o_ref[0, rs, :] + e1 * o_ref[1, rs, :] + e2 * o_ref[2, rs, :]) / (e0 + e1 + e2)
        y_ref[0, rs, :] = (mix * zg_ref[0, rs, :].astype(F32)).astype(BF16)


def _dilated(views, zg3, slopes):
    _, bsz, heads, seq, hd = views[0].shape
    specs = [pl.BlockSpec((3, 1, 1) + v.shape[3:], lambda b, h: (0, b, h, 0, 0)) for v in views]
    tok_spec = pl.BlockSpec((1, seq, hd), lambda b, h: (b, 0, h))
    return pl.pallas_call(
        _dil_kernel,
        grid=(bsz, heads),
        in_specs=[pl.BlockSpec(memory_space=pltpu.SMEM)] + specs + [tok_spec],
        out_specs=tok_spec,
        out_shape=jax.ShapeDtypeStruct((bsz, seq, heads * hd), BF16),
        scratch_shapes=[pltpu.VMEM((3, seq, hd), F32), pltpu.VMEM((3, seq, hd), F32)],
        compiler_params=pltpu.CompilerParams(dimension_semantics=("arbitrary", "arbitrary"),
                                             vmem_limit_bytes=VMEM_LIMIT),
        name="dilated_mix",
    )(slopes, *views, zg3)


def _rope(pe, cc, sa, sb):
    return pe * cc + pltpu.roll(pe, 96, 1) * sa + pltpu.roll(pe, 32, 1) * sb


def _mla_proj_kernel(h_ref, wmid_ref, wbz_ref, wkpe_ref, gq_ref, gkv_ref, wq_ref, wkv_ref,
                     cc_ref, sa_ref, sb_ref, zg_ref, bzg_ref, qf_ref, kf_ref, v_ref):
    def proj(w_ref, off, width):
        return jnp.dot(h_ref[...], w_ref[:, off:off + width], preferred_element_type=F32)

    chunk = 4 * LANES
    for c0 in range(0, A_WIDTH, chunk):
        zg_ref[:, c0:c0 + chunk] = _silu(proj(wmid_ref, c0, chunk)).astype(BF16)
    for c0 in range(0, B_WIDTH, chunk):
        bzg_ref[:, c0:c0 + chunk] = _silu(proj(wbz_ref, c0, chunk)).astype(BF16)

    cc, sa, sb = cc_ref[...], sa_ref[...], sb_ref[...]
    q_scale = (B_NOPE_DIM + B_ROPE_DIM) ** -0.5 * LOG2E
    cqn = _rms(proj(wmid_ref, OFF_CQ - OFF_AZ, Q_LORA_RANK), gq_ref[...]).astype(BF16)
    for h in range(B_HEADS):
        q = jnp.dot(cqn, wq_ref[:, h * B_QK_PAD:(h + 1) * B_QK_PAD], preferred_element_type=F32)
        qf_ref[0, h, :, :B_NOPE_DIM] = (q[:, :B_NOPE_DIM] * q_scale).astype(BF16)
        qf_ref[0, h, :, B_NOPE_DIM:] = (_rope(q[:, B_NOPE_DIM:], cc, sa, sb) * q_scale).astype(BF16)
    k_rot = _rope(proj(wkpe_ref, 0, KPE_PAD), cc, sa, sb).astype(BF16)
    ckvn = _rms(proj(wmid_ref, OFF_CKV - OFF_AZ, KV_LORA_RANK), gkv_ref[...]).astype(BF16)
    kv_w = B_NOPE_DIM + B_V_DIM
    for h in range(B_HEADS):
        kv = jnp.dot(ckvn, wkv_ref[:, h * kv_w:(h + 1) * kv_w], preferred_element_type=F32)
        kf_ref[0, h, :, :B_NOPE_DIM] = kv[:, :B_NOPE_DIM].astype(BF16)
        kf_ref[0, h, :, B_NOPE_DIM:] = k_rot
        v_ref[0, h, :, :] = kv[:, B_NOPE_DIM:].astype(BF16)


def _mla_proj(h, w_mid, w_bz, w_kpe, g_q, g_kv, wq_p, wkv, cc, sa, sb, bsz, seq):
    n_tok, d = h.shape
    tm = ROW_TILE
    per_b = seq // tm
    row = lambda i: (i, 0)
    full = lambda i: (0, 0)
    pos = lambda i: (i % per_b, 0)
    head_major = lambda i: (i // per_b, 0, i % per_b, 0)
    resident = lambda w: pl.BlockSpec(w.shape, full, pipeline_mode=pl.Buffered(1))
    return pl.pallas_call(
        _mla_proj_kernel,
        grid=(n_tok // tm,),
        in_specs=[
            pl.BlockSpec((tm, d), row),
            resident(w_mid), resident(w_bz), resident(w_kpe),
            pl.BlockSpec((1, Q_LORA_RANK), full),
            pl.BlockSpec((1, KV_LORA_RANK), full),
            resident(wq_p), resident(wkv),
            pl.BlockSpec((tm, LANES), pos),
            pl.BlockSpec((tm, LANES), pos),
            pl.BlockSpec((tm, LANES), pos),
        ],
        out_specs=[
            pl.BlockSpec((tm, A_WIDTH), row),
            pl.BlockSpec((tm, B_WIDTH), row),
            pl.BlockSpec((1, B_HEADS, tm, B_QK_PAD), head_major),
            pl.BlockSpec((1, B_HEADS, tm, B_QK_PAD), head_major),
            pl.BlockSpec((1, B_HEADS, tm, B_V_DIM), head_major),
        ],
        out_shape=[
            jax.ShapeDtypeStruct((n_tok, A_WIDTH), BF16),
            jax.ShapeDtypeStruct((n_tok, B_WIDTH), BF16),
            jax.ShapeDtypeStruct((bsz, B_HEADS, seq, B_QK_PAD), BF16),
            jax.ShapeDtypeStruct((bsz, B_HEADS, seq, B_QK_PAD), BF16),
            jax.ShapeDtypeStruct((bsz, B_HEADS, seq, B_V_DIM), BF16),
        ],
        compiler_params=pltpu.CompilerParams(dimension_semantics=("arbitrary",),
                                             vmem_limit_bytes=VMEM_LIMIT),
        name="mla_proj",
    )(h, w_mid, w_bz, w_kpe, g_q.reshape(1, -1), g_kv.reshape(1, -1), wq_p, wkv, cc, sa, sb)


def _mla_attn_kernel(q_ref, k_ref, v_ref, zg_ref, y_ref, vext_ref, s_ref):
    seq = k_ref.shape[2]
    n_blk = seq // MLA_Q
    vext_ref[:, :B_V_DIM] = v_ref[0, 0, :, :]
    vext_ref[:, B_V_DIM:] = jnp.ones((seq, B_V_DIM), BF16)

    def rows_of(blk):
        start = blk * MLA_Q
        return pl.ds(start if isinstance(blk, int) else pl.multiple_of(start, MLA_Q), MLA_Q)

    def scores(blk, slot):
        s_ref[slot] = _dot_nt(q_ref[0, 0, rows_of(blk), :], k_ref[0, 0, :, :])

    def softmax_pv(blk, slot):
        s = s_ref[slot]
        m = jnp.max(s, axis=-1, keepdims=True)
        p = jnp.exp2(s - m).astype(BF16)
        ol = jnp.dot(p, vext_ref[...], preferred_element_type=F32)
        o = ol[:, :B_V_DIM] / ol[:, B_V_DIM:]
        rows = rows_of(blk)
        y_ref[0, rows, :] = (o * zg_ref[0, rows, :].astype(F32)).astype(BF16)

    scores(0, 0)

    def body(j, carry):
        blk = 2 * j
        scores(blk + 1, 1)
        softmax_pv(blk, 0)
        scores(blk + 2, 0)
        softmax_pv(blk + 1, 1)
        return carry

    lax.fori_loop(0, n_blk // 2 - 1, body, 0)
    scores(n_blk - 1, 1)
    softmax_pv(n_blk - 2, 0)
    softmax_pv(n_blk - 1, 1)


def _mla_attn(qf, kf, v, bzg3):
    bsz, heads, seq, _ = qf.shape
    head_blk = lambda b, h: (b, h, 0, 0)
    tok_spec = pl.BlockSpec((1, seq, B_V_DIM), lambda b, h: (b, 0, h))
    return pl.pallas_call(
        _mla_attn_kernel,
        grid=(bsz, heads),
        in_specs=[
            pl.BlockSpec((1, 1, seq, B_QK_PAD), head_blk),
            pl.BlockSpec((1, 1, seq, B_QK_PAD), head_blk),
            pl.BlockSpec((1, 1, seq, B_V_DIM), head_blk),
            tok_spec,
        ],
        out_specs=tok_spec,
        out_shape=jax.ShapeDtypeStruct((bsz, seq, heads * B_V_DIM), BF16),
        scratch_shapes=[pltpu.VMEM((seq, 2 * B_V_DIM), BF16), pltpu.VMEM((2, MLA_Q, seq), F32)],
        compiler_params=pltpu.CompilerParams(dimension_semantics=("arbitrary", "arbitrary"),
                                             vmem_limit_bytes=VMEM_LIMIT),
        name="mla_attn",
    )(qf, kf, v, bzg3)


def _out_kernel(ya_ref, yb_ref, w_ref, x_ref, mod_ref, g_ref, o_ref):
    y = (jnp.dot(ya_ref[...], w_ref[:A_WIDTH, :], preferred_element_type=F32)
         + jnp.dot(yb_ref[...], w_ref[A_WIDTH:, :], preferred_element_type=F32))
    gate = mod_ref[0, 2:3, :]
    o_ref[...] = x_ref[...] + gate * _rms(y, g_ref[...])


def _out_proj(ya, yb, w_out, x2, mod3, g_post, seq):
    n_tok, d = x2.shape
    tm = ROW_TILE
    per_b = seq // tm
    row = lambda i: (i, 0)
    return pl.pallas_call(
        _out_kernel,
        grid=(n_tok // tm,),
        in_specs=[
            pl.BlockSpec((tm, A_WIDTH), row),
            pl.BlockSpec((tm, B_WIDTH), row),
            pl.BlockSpec(w_out.shape, lambda i: (0, 0), pipeline_mode=pl.Buffered(1)),
            pl.BlockSpec((tm, d), row),
            pl.BlockSpec((1, 3, d), lambda i: (i // per_b, 0, 0)),
            pl.BlockSpec((1, d), lambda i: (0, 0)),
        ],
        out_specs=pl.BlockSpec((tm, d), row),
        out_shape=jax.ShapeDtypeStruct((n_tok, d), F32),
        compiler_params=pltpu.CompilerParams(dimension_semantics=("arbitrary",),
                                             vmem_limit_bytes=VMEM_LIMIT),
        name="out_proj",
    )(ya, yb, w_out, x2, mod3, g_post.reshape(1, d))


def _pad_w_uq(w_uq):
    r = w_uq.shape[0]
    w = w_uq.reshape(r, B_HEADS, B_NOPE_DIM + B_ROPE_DIM)
    pad = jnp.zeros((r, B_HEADS, B_QK_PAD - B_NOPE_DIM - B_ROPE_DIM), w_uq.dtype)
    return jnp.concatenate([w, pad], axis=2).reshape(r, B_HEADS * B_QK_PAD).astype(BF16)


def _rope_tables(seq):
    half = B_ROPE_DIM // 2
    pos = jnp.arange(seq, dtype=F32)
    inv_freq = jnp.power(ROPE_THETA, -jnp.arange(half, dtype=F32) / half)
    ang = pos[:, None] * inv_freq[None, :]
    cos, sin = jnp.cos(ang), jnp.sin(ang)
    z = jnp.zeros_like(cos)
    cc = jnp.concatenate([cos, cos, z, z], axis=1)
    sa = jnp.concatenate([-sin, z, z, z], axis=1)
    sb = jnp.concatenate([z, sin, z, z], axis=1)
    return cc, sa, sb


def _layer(x, c, w_ada, b_ada, g_pre, w_in, g_q_lora, w_uq, g_kv_lora, w_ukv, w_out, g_post):
    bsz, seq, d = x.shape
    x2 = x.reshape(bsz * seq, d)
    mod3 = _ada(c, w_ada, b_ada).reshape(bsz, 3, d)
    w_qkv, w_mid, w_bz, w_kpe = _cast_w_in(w_in)
    h, s1, s4, s16 = _qkv_proj(x2, mod3, g_pre, w_qkv, bsz, seq)
    cc, sa, sb = _rope_tables(seq)
    zg, bzg, qf, kf, v = _mla_proj(h, w_mid, w_bz, w_kpe, g_q_lora, g_kv_lora, _pad_w_uq(w_uq),
                                   w_ukv.astype(BF16), cc, sa, sb, bsz, seq)
    slopes = jnp.exp2(-8.0 * jnp.arange(1, A_HEADS + 1, dtype=F32) / A_HEADS)
    ya = _dilated((s1, s4, s16), zg.reshape(bsz, seq, A_WIDTH), slopes)
    yb = _mla_attn(qf, kf, v, bzg.reshape(bsz, seq, B_WIDTH))
    out = _out_proj(ya.reshape(bsz * seq, A_WIDTH), yb.reshape(bsz * seq, B_WIDTH),
                    w_out.astype(BF16), x2, mod3, g_post, seq)
    return out.reshape(bsz, seq, d)


def kernel(x, c, w_ada, b_ada, g_pre, w_in, g_q_lora, w_uq, g_kv_lora, w_ukv, w_out, g_post):
    for layer in range(w_ada.shape[0]):
        x = _layer(x, c, w_ada[layer], b_ada[layer], g_pre[layer], w_in[layer], g_q_lora[layer],
                   w_uq[layer], g_kv_lora[layer], w_ukv[layer], w_out[layer], g_post[layer])
    return x
```

```python
import jax
import jax.numpy as jnp
from jax import lax
from jax.experimental import pallas as pl
from jax.experimental.pallas import tpu as pltpu

F32 = jnp.float32
BF16 = jnp.bfloat16

D_MODEL = 2048
A_HEADS = 8
A_HEAD_DIM = 128
A_WIDTH = A_HEADS * A_HEAD_DIM
DILATED_PATTERNS = ((128, 1), (512, 4), (2048, 16))
B_HEADS = 8
B_V_DIM = 128
B_WIDTH = B_HEADS * B_V_DIM
B_NOPE_DIM = 128
B_ROPE_DIM = 64
Q_LORA_RANK = 512
KV_LORA_RANK = 256
ROPE_THETA = 10000.0
NORM_EPS = 1e-6
NEG_INF = -1e30
LOG2E = 1.4426950408889634

LANES = 128
B_QK_PAD = 256
KPE_PAD = LANES

OFF_AZ = 3 * A_WIDTH
OFF_CQ = OFF_AZ + A_WIDTH
OFF_CKV = OFF_CQ + Q_LORA_RANK

ROW_TILE = 512
BAND_Q = 128
MLA_Q = 256
MLA_HEAD_GROUP = 2
VMEM_LIMIT = 56 * 1024 * 1024


def _silu(v):
    return v * (1.0 / (1.0 + jnp.exp(-v)))


def _rms(v, g):
    return v * lax.rsqrt(jnp.mean(v * v, axis=-1, keepdims=True) + NORM_EPS) * g


def _dot_nt(a, b):
    return lax.dot_general(a, b, (((1,), (1,)), ((), ())), preferred_element_type=F32)


def _ada_kernel(c_ref, w_ref, b_ref, o_ref):
    s = _silu(c_ref[...]).astype(BF16)
    o_ref[...] = jnp.dot(s, w_ref[...].astype(BF16), preferred_element_type=F32) + b_ref[...]


def _ada(c, w_ada, b_ada):
    bsz, d = c.shape
    n = w_ada.shape[1]
    tn = 512
    return pl.pallas_call(
        _ada_kernel,
        grid=(n // tn,),
        in_specs=[
            pl.BlockSpec((bsz, d), lambda j: (0, 0)),
            pl.BlockSpec((d, tn), lambda j: (0, j)),
            pl.BlockSpec((1, tn), lambda j: (0, j)),
        ],
        out_specs=pl.BlockSpec((bsz, tn), lambda j: (0, j)),
        out_shape=jax.ShapeDtypeStruct((bsz, n), F32),
        compiler_params=pltpu.CompilerParams(dimension_semantics=("arbitrary",)),
        name="ada_mod",
    )(c, w_ada, b_ada.reshape(1, n))


def _cast_w_in_kernel(w_ref, qkv_ref, mid_ref, bz_ref, kpe_ref):
    kpe0 = OFF_CKV + KV_LORA_RANK
    qkv_ref[...] = w_ref[:OFF_AZ, :].astype(BF16)
    mid_ref[...] = w_ref[OFF_AZ:kpe0, :].astype(BF16)
    bz_ref[...] = w_ref[kpe0 + B_ROPE_DIM:, :].astype(BF16)
    kpe_ref[:B_ROPE_DIM, :] = w_ref[kpe0:kpe0 + B_ROPE_DIM, :].astype(BF16)
    kpe_ref[B_ROPE_DIM:, :] = jnp.zeros((KPE_PAD - B_ROPE_DIM, w_ref.shape[1]), BF16)


def _cast_w_in(w_in_t):
    n, d = w_in_t.shape
    tc = 256
    heights = (OFF_AZ, OFF_CKV + KV_LORA_RANK - OFF_AZ, B_WIDTH, KPE_PAD)
    return pl.pallas_call(
        _cast_w_in_kernel,
        grid=(d // tc,),
        in_specs=[pl.BlockSpec((n, tc), lambda j: (0, j))],
        out_specs=[pl.BlockSpec((hgt, tc), lambda j: (0, j)) for hgt in heights],
        out_shape=[jax.ShapeDtypeStruct((hgt, d), BF16) for hgt in heights],
        compiler_params=pltpu.CompilerParams(dimension_semantics=("arbitrary",),
                                             vmem_limit_bytes=VMEM_LIMIT),
        name="cast_w_in",
    )(w_in_t)


def _qkv_kernel(x_ref, mod_ref, g_ref, w_ref, h_ref, s1_ref, s4_ref, s16_ref, acc_ref, acc4_ref):
    tm = x_ref.shape[0]
    d4 = DILATED_PATTERNS[1][1]
    x = x_ref[...]
    gain = g_ref[...] * (1.0 + mod_ref[0, 1:2, :])
    inv = lax.rsqrt(jnp.mean(x * x, axis=-1, keepdims=True) + NORM_EPS)
    h_ref[...] = (x * inv * gain + mod_ref[0, 0:1, :]).astype(BF16)

    heads_per = acc_ref.shape[0]
    chunk = heads_per * A_HEAD_DIM
    q_scale = A_HEAD_DIM ** -0.5 * LOG2E
    for t in range(3):
        for c0 in range(0, A_WIDTH, chunk):
            acc = _dot_nt(h_ref[...], w_ref[t * A_WIDTH + c0:t * A_WIDTH + c0 + chunk, :])
            if t == 0:
                acc = acc * q_scale
            for k in range(heads_per):
                head = c0 // A_HEAD_DIM + k
                a = acc[:, k * A_HEAD_DIM:(k + 1) * A_HEAD_DIM]
                s1_ref[t, 0, head, :, :] = a.astype(BF16)
                acc_ref[k, :, :] = a
            for k in range(heads_per):
                head = c0 // A_HEAD_DIM + k
                for r4 in range(d4):
                    g4 = acc_ref[k, pl.ds(r4, tm // d4, stride=d4), :]
                    s4_ref[t, 0, head, :, r4 * LANES:(r4 + 1) * LANES] = g4.astype(BF16)
                    acc4_ref[k, r4, :, :] = g4
                for r4 in range(d4):
                    for a4 in range(d4):
                        r = r4 + d4 * a4
                        s16_ref[t, 0, head, :, r * LANES:(r + 1) * LANES] = (
                            acc4_ref[k, r4, pl.ds(a4, tm // (d4 * d4), stride=d4), :].astype(BF16))


def _qkv_proj(x2, mod3, g_pre, w_qkv, bsz, seq):
    n_tok, d = x2.shape
    tm = ROW_TILE
    per_b = seq // tm
    hd = A_HEAD_DIM
    d4, d16 = DILATED_PATTERNS[1][1], DILATED_PATTERNS[2][1]
    head_major = lambda i: (0, i // per_b, 0, i % per_b, 0)
    return pl.pallas_call(
        _qkv_kernel,
        grid=(n_tok // tm,),
        in_specs=[
            pl.BlockSpec((tm, d), lambda i: (i, 0)),
            pl.BlockSpec((1, 3, d), lambda i: (i // per_b, 0, 0)),
            pl.BlockSpec((1, d), lambda i: (0, 0)),
            pl.BlockSpec(w_qkv.shape, lambda i: (0, 0), pipeline_mode=pl.Buffered(1)),
        ],
        out_specs=[
            pl.BlockSpec((tm, d), lambda i: (i, 0)),
            pl.BlockSpec((3, 1, A_HEADS, tm, hd), head_major),
            pl.BlockSpec((3, 1, A_HEADS, tm // d4, d4 * hd), head_major),
            pl.BlockSpec((3, 1, A_HEADS, tm // d16, d16 * hd), head_major),
        ],
        out_shape=[
            jax.ShapeDtypeStruct((n_tok, d), BF16),
            jax.ShapeDtypeStruct((3, bsz, A_HEADS, seq, hd), BF16),
            jax.ShapeDtypeStruct((3, bsz, A_HEADS, seq // d4, d4 * hd), BF16),
            jax.ShapeDtypeStruct((3, bsz, A_HEADS, seq // d16, d16 * hd), BF16),
        ],
        scratch_shapes=[pltpu.VMEM((4, tm, hd), F32), pltpu.VMEM((4, d4, tm // d4, hd), F32)],
        compiler_params=pltpu.CompilerParams(dimension_semantics=("arbitrary",),
                                             vmem_limit_bytes=VMEM_LIMIT),
        name="qkv_proj",
    )(x2, mod3, g_pre.reshape(1, d), w_qkv)


def _band_bias(rows, cols, k_minus_q, radius, pen):
    rel = (lax.broadcasted_iota(jnp.int32, (rows, cols), 1) + k_minus_q
           - lax.broadcasted_iota(jnp.int32, (rows, cols), 0))
    dist = jnp.abs(rel)
    return jnp.where(dist <= radius, -pen * dist.astype(F32), NEG_INF)


def _band_block(q, k_win, v_win, bias):
    s = _dot_nt(q, k_win) + bias
    m = jnp.max(s, axis=-1, keepdims=True)
    p = jnp.exp2(s - m).astype(BF16)
    v_ext = jnp.concatenate([v_win, jnp.ones_like(v_win)], axis=1)
    ol = jnp.dot(p, v_ext, preferred_element_type=F32)
    l = ol[:, A_HEAD_DIM:]
    o = ol[:, :A_HEAD_DIM] / l
    return o, m + jnp.log2(l)


def _dil_kernel(slope_ref, s1_ref, s4_ref, s16_ref, zg_ref, y_ref, o_ref, lse_ref):
    seq = s1_ref.shape[3]
    slope = slope_ref[pl.program_id(1)]
    bq = BAND_Q
    win = 2 * bq
    views = (s1_ref, s4_ref, s16_ref)

    for p_idx, (window, dil) in enumerate(DILATED_PATTERNS):
        radius = window // 2 // dil
        length = seq // dil
        ref = views[p_idx]
        pen = slope * (float(dil) * LOG2E)
        n_blk = length // bq

        def store(res, lo, r, p_idx=p_idx, dil=dil):
            o, lse = res
            if dil == 1:
                rows = pl.ds(lo, bq)
            else:
                rows = pl.ds(lo * dil + r, bq, stride=dil)
            o_ref[p_idx, rows, :] = o
            lse_ref[p_idx, rows, :] = lse

        if n_blk == 1:
            bias = _band_bias(bq, bq, 0, radius, pen)
            for r in range(dil):
                cols = slice(r * LANES, (r + 1) * LANES)
                store(_band_block(ref[0, 0, 0, :, cols], ref[1, 0, 0, :, cols],
                                  ref[2, 0, 0, :, cols], bias), 0, r)
            continue

        bias_first = _band_bias(bq, win, 0, radius, pen)
        bias_mid = _band_bias(bq, win, -radius, radius, pen)
        bias_last = _band_bias(bq, win, -bq, radius, pen)
        for r in range(dil):
            cols = slice(r * LANES, (r + 1) * LANES)

            def block(q_lo, k_lo, bias, ref=ref, cols=cols):
                return _band_block(ref[0, 0, 0, pl.ds(q_lo, bq), cols],
                                   ref[1, 0, 0, pl.ds(k_lo, win), cols],
                                   ref[2, 0, 0, pl.ds(k_lo, win), cols], bias)

            store(block(0, 0, bias_first), 0, r)
            store(block(length - bq, length - win, bias_last), length - bq, r)
            for qi in range(1, n_blk - 1):
                store(block(qi * bq, qi * bq - radius, bias_mid), qi * bq, r)

    rows_per = 256
    for c0 in range(0, seq, rows_per):
        rs = slice(c0, c0 + rows_per)
        l0, l1, l2 = lse_ref[0, rs, :], lse_ref[1, rs, :], lse_ref[2, rs, :]
        m = jnp.maximum(jnp.maximum(l0, l1), l2)
        e0, e1, e2 = jnp.exp2(l0 - m), jnp.exp2(l1 - m), jnp.exp2(l2 - m)
        mix = (e0 * o_ref[0, rs, :] + e1 * o_ref[1, rs, :] + e2 * o_ref[2, rs, :]) / (e0 + e1 + e2)
        y_ref[0, rs, :] = (mix * zg_ref[0, rs, :].astype(F32)).astype(BF16)


def _dilated(views, zg3, slopes):
    _, bsz, heads, seq, hd = views[0].shape
    specs = [pl.BlockSpec((3, 1, 1) + v.shape[3:], lambda b, h: (0, b, h, 0, 0)) for v in views]
    tok_spec = pl.BlockSpec((1, seq, hd), lambda b, h: (b, 0, h))
    return pl.pallas_call(
        _dil_kernel,
        grid=(bsz, heads),
        in_specs=[pl.BlockSpec(memory_space=pltpu.SMEM)] + specs + [tok_spec],
        out_specs=tok_spec,
        out_shape=jax.ShapeDtypeStruct((bsz, seq, heads * hd), BF16),
        scratch_shapes=[pltpu.VMEM((3, seq, hd), F32), pltpu.VMEM((3, seq, hd), F32)],
        compiler_params=pltpu.CompilerParams(dimension_semantics=("arbitrary", "arbitrary"),
                                             vmem_limit_bytes=VMEM_LIMIT),
        name="dilated_mix",
    )(slopes, *views, zg3)


def _rope(pe, cc, sa, sb):
    return pe * cc + pltpu.roll(pe, 96, 1) * sa + pltpu.roll(pe, 32, 1) * sb


def _mla_proj_kernel(h_ref, wmid_ref, wbz_ref, wkpe_ref, gq_ref, gkv_ref, wq_ref, wkv_ref,
                     cc_ref, sa_ref, sb_ref, zg_ref, bzg_ref, qf_ref, kf_ref, v_ref):
    def proj(w_ref, off, width):
        return _dot_nt(h_ref[...], w_ref[off:off + width, :])

    chunk = 4 * LANES
    for c0 in range(0, A_WIDTH, chunk):
        zg_ref[:, c0:c0 + chunk] = _silu(proj(wmid_ref, c0, chunk)).astype(BF16)
    for c0 in range(0, B_WIDTH, chunk):
        bzg_ref[:, c0:c0 + chunk] = _silu(proj(wbz_ref, c0, chunk)).astype(BF16)

    cc, sa, sb = cc_ref[...], sa_ref[...], sb_ref[...]
    q_scale = (B_NOPE_DIM + B_ROPE_DIM) ** -0.5 * LOG2E
    cqn = _rms(proj(wmid_ref, OFF_CQ - OFF_AZ, Q_LORA_RANK), gq_ref[...]).astype(BF16)
    for h in range(B_HEADS):
        q = jnp.dot(cqn, wq_ref[:, h * B_QK_PAD:(h + 1) * B_QK_PAD], preferred_element_type=F32)
        qf_ref[0, h, :, :B_NOPE_DIM] = (q[:, :B_NOPE_DIM] * q_scale).astype(BF16)
        qf_ref[0, h, :, B_NOPE_DIM:] = (_rope(q[:, B_NOPE_DIM:], cc, sa, sb) * q_scale).astype(BF16)
    k_rot = _rope(proj(wkpe_ref, 0, KPE_PAD), cc, sa, sb).astype(BF16)
    ckvn = _rms(proj(wmid_ref, OFF_CKV - OFF_AZ, KV_LORA_RANK), gkv_ref[...]).astype(BF16)
    kv_w = B_NOPE_DIM + B_V_DIM
    for h in range(B_HEADS):
        kv = jnp.dot(ckvn, wkv_ref[:, h * kv_w:(h + 1) * kv_w], preferred_element_type=F32)
        kf_ref[0, h, :, :B_NOPE_DIM] = kv[:, :B_NOPE_DIM].astype(BF16)
        kf_ref[0, h, :, B_NOPE_DIM:] = k_rot
        v_ref[0, h, :, :] = kv[:, B_NOPE_DIM:].astype(BF16)


def _mla_proj(h, w_mid, w_bz, w_kpe, g_q, g_kv, wq_p, wkv, cc, sa, sb, bsz, seq):
    n_tok, d = h.shape
    tm = ROW_TILE
    per_b = seq // tm
    row = lambda i: (i, 0)
    full = lambda i: (0, 0)
    pos = lambda i: (i % per_b, 0)
    head_major = lambda i: (i // per_b, 0, i % per_b, 0)
    resident = lambda w: pl.BlockSpec(w.shape, full, pipeline_mode=pl.Buffered(1))
    return pl.pallas_call(
        _mla_proj_kernel,
        grid=(n_tok // tm,),
        in_specs=[
            pl.BlockSpec((tm, d), row),
            resident(w_mid), resident(w_bz), resident(w_kpe),
            pl.BlockSpec((1, Q_LORA_RANK), full),
            pl.BlockSpec((1, KV_LORA_RANK), full),
            resident(wq_p), resident(wkv),
            pl.BlockSpec((tm, LANES), pos),
            pl.BlockSpec((tm, LANES), pos),
            pl.BlockSpec((tm, LANES), pos),
        ],
        out_specs=[
            pl.BlockSpec((tm, A_WIDTH), row),
            pl.BlockSpec((tm, B_WIDTH), row),
            pl.BlockSpec((1, B_HEADS, tm, B_QK_PAD), head_major),
            pl.BlockSpec((1, B_HEADS, tm, B_QK_PAD), head_major),
            pl.BlockSpec((1, B_HEADS, tm, B_V_DIM), head_major),
        ],
        out_shape=[
            jax.ShapeDtypeStruct((n_tok, A_WIDTH), BF16),
            jax.ShapeDtypeStruct((n_tok, B_WIDTH), BF16),
            jax.ShapeDtypeStruct((bsz, B_HEADS, seq, B_QK_PAD), BF16),
            jax.ShapeDtypeStruct((bsz, B_HEADS, seq, B_QK_PAD), BF16),
            jax.ShapeDtypeStruct((bsz, B_HEADS, seq, B_V_DIM), BF16),
        ],
        compiler_params=pltpu.CompilerParams(dimension_semantics=("arbitrary",),
                                             vmem_limit_bytes=VMEM_LIMIT),
        name="mla_proj",
    )(h, w_mid, w_bz, w_kpe, g_q.reshape(1, -1), g_kv.reshape(1, -1), wq_p, wkv, cc, sa, sb)


def _mla_attn_kernel(q_ref, k_ref, v_ref, zg_ref, y_ref, vext_ref, s_ref):
    heads = q_ref.shape[1]
    seq = k_ref.shape[2]
    n_blk = seq // MLA_Q
    for hh in range(heads):
        vext_ref[hh, :, :B_V_DIM] = v_ref[0, hh, :, :]
        vext_ref[hh, :, B_V_DIM:] = jnp.ones((seq, B_V_DIM), BF16)

    def rows_of(blk):
        start = blk * MLA_Q
        return pl.ds(start if isinstance(blk, int) else pl.multiple_of(start, MLA_Q), MLA_Q)

    def scores(hh, blk, slot):
        s_ref[slot] = _dot_nt(q_ref[0, hh, rows_of(blk), :], k_ref[0, hh, :, :])

    def softmax_pv(hh, blk, slot):
        s = s_ref[slot]
        m = jnp.max(s, axis=-1, keepdims=True)
        p = jnp.exp2(s - m).astype(BF16)
        ol = jnp.dot(p, vext_ref[hh], preferred_element_type=F32)
        o = ol[:, :B_V_DIM] / ol[:, B_V_DIM:]
        rows = rows_of(blk)
        cols = slice(hh * B_V_DIM, (hh + 1) * B_V_DIM)
        y_ref[0, rows, cols] = (o * zg_ref[0, rows, cols].astype(F32)).astype(BF16)

    items = [(hh, blk) for hh in range(heads) for blk in range(n_blk)]
    scores(*items[0], 0)
    for n, item in enumerate(items):
        if n + 1 < len(items):
            scores(*items[n + 1], (n + 1) % 2)
        softmax_pv(*item, n % 2)


def _mla_attn(qf, kf, v, bzg3):
    bsz, heads, seq, _ = qf.shape
    grp = MLA_HEAD_GROUP
    head_blk = lambda b, g: (b, g, 0, 0)
    tok_spec = pl.BlockSpec((1, seq, grp * B_V_DIM), lambda b, g: (b, 0, g))
    return pl.pallas_call(
        _mla_attn_kernel,
        grid=(bsz, heads // grp),
        in_specs=[
            pl.BlockSpec((1, grp, seq, B_QK_PAD), head_blk),
            pl.BlockSpec((1, grp, seq, B_QK_PAD), head_blk),
            pl.BlockSpec((1, grp, seq, B_V_DIM), head_blk),
            tok_spec,
        ],
        out_specs=tok_spec,
        out_shape=jax.ShapeDtypeStruct((bsz, seq, heads * B_V_DIM), BF16),
        scratch_shapes=[pltpu.VMEM((grp, seq, 2 * B_V_DIM), BF16), pltpu.VMEM((2, MLA_Q, seq), F32)],
        compiler_params=pltpu.CompilerParams(dimension_semantics=("arbitrary", "arbitrary"),
                                             vmem_limit_bytes=VMEM_LIMIT),
        name="mla_attn",
    )(qf, kf, v, bzg3)


def _out_kernel(ya_ref, yb_ref, w_ref, x_ref, mod_ref, g_ref, o_ref):
    y = (jnp.dot(ya_ref[...], w_ref[:A_WIDTH, :], preferred_element_type=F32)
         + jnp.dot(yb_ref[...], w_ref[A_WIDTH:, :], preferred_element_type=F32))
    gate = mod_ref[0, 2:3, :]
    o_ref[...] = x_ref[...] + gate * _rms(y, g_ref[...])


def _out_proj(ya, yb, w_out, x2, mod3, g_post, seq):
    n_tok, d = x2.shape
    tm = ROW_TILE
    per_b = seq // tm
    row = lambda i: (i, 0)
    return pl.pallas_call(
        _out_kernel,
        grid=(n_tok // tm,),
        in_specs=[
            pl.BlockSpec((tm, A_WIDTH), row),
            pl.BlockSpec((tm, B_WIDTH), row),
            pl.BlockSpec(w_out.shape, lambda i: (0, 0), pipeline_mode=pl.Buffered(1)),
            pl.BlockSpec((tm, d), row),
            pl.BlockSpec((1, 3, d), lambda i: (i // per_b, 0, 0)),
            pl.BlockSpec((1, d), lambda i: (0, 0)),
        ],
        out_specs=pl.BlockSpec((tm, d), row),
        out_shape=jax.ShapeDtypeStruct((n_tok, d), F32),
        compiler_params=pltpu.CompilerParams(dimension_semantics=("arbitrary",),
                                             vmem_limit_bytes=VMEM_LIMIT),
        name="out_proj",
    )(ya, yb, w_out, x2, mod3, g_post.reshape(1, d))


def _pad_w_uq(w_uq):
    r = w_uq.shape[0]
    w = w_uq.reshape(r, B_HEADS, B_NOPE_DIM + B_ROPE_DIM)
    pad = jnp.zeros((r, B_HEADS, B_QK_PAD - B_NOPE_DIM - B_ROPE_DIM), w_uq.dtype)
    return jnp.concatenate([w, pad], axis=2).reshape(r, B_HEADS * B_QK_PAD).astype(BF16)


def _rope_tables(seq):
    half = B_ROPE_DIM // 2
    pos = jnp.arange(seq, dtype=F32)
    inv_freq = jnp.power(ROPE_THETA, -jnp.arange(half, dtype=F32) / half)
    ang = pos[:, None] * inv_freq[None, :]
    cos, sin = jnp.cos(ang), jnp.sin(ang)
    z = jnp.zeros_like(cos)
    cc = jnp.concatenate([cos, cos, z, z], axis=1)
    sa = jnp.concatenate([-sin, z, z, z], axis=1)
    sb = jnp.concatenate([z, sin, z, z], axis=1)
    return cc, sa, sb


def _layer(x, c, w_ada, b_ada, g_pre, w_in, g_q_lora, w_uq, g_kv_lora, w_ukv, w_out, g_post):
    bsz, seq, d = x.shape
    x2 = x.reshape(bsz * seq, d)
    mod3 = _ada(c, w_ada, b_ada).reshape(bsz, 3, d)
    w_qkv, w_mid, w_bz, w_kpe = _cast_w_in(w_in.T)
    h, s1, s4, s16 = _qkv_proj(x2, mod3, g_pre, w_qkv, bsz, seq)
    cc, sa, sb = _rope_tables(seq)
    zg, bzg, qf, kf, v = _mla_proj(h, w_mid, w_bz, w_kpe, g_q_lora, g_kv_lora, _pad_w_uq(w_uq),
                                   w_ukv.astype(BF16), cc, sa, sb, bsz, seq)
    slopes = jnp.exp2(-8.0 * jnp.arange(1, A_HEADS + 1, dtype=F32) / A_HEADS)
    ya = _dilated((s1, s4, s16), zg.reshape(bsz, seq, A_WIDTH), slopes)
    yb = _mla_attn(qf, kf, v, bzg.reshape(bsz, seq, B_WIDTH))
    out = _out_proj(ya.reshape(bsz * seq, A_WIDTH), yb.reshape(bsz * seq, B_WIDTH),
                    w_out.astype(BF16), x2, mod3, g_post, seq)
    return out.reshape(bsz, seq, d)


def kernel(x, c, w_ada, b_ada, g_pre, w_in, g_q_lora, w_uq, g_kv_lora, w_ukv, w_out, g_post):
    for layer in range(w_ada.shape[0]):
        x = _layer(x, c, w_ada[layer], b_ada[layer], g_pre[layer], w_in[layer], g_q_lora[layer],
                   w_uq[layer], g_kv_lora[layer], w_ukv[layer], w_out[layer], g_post[layer])
    return x
```

```python
import jax
import jax.numpy as jnp
import numpy as np
from jax import lax
from jax.experimental import pallas as pl
from jax.experimental.pallas import tpu as pltpu

F32 = jnp.float32
BF16 = jnp.bfloat16

D_MODEL = 2048
A_HEADS = 8
A_HEAD_DIM = 128
A_WIDTH = A_HEADS * A_HEAD_DIM
DILATED_PATTERNS = ((128, 1), (512, 4), (2048, 16))
B_HEADS = 8
B_V_DIM = 128
B_WIDTH = B_HEADS * B_V_DIM
B_NOPE_DIM = 128
B_ROPE_DIM = 64
Q_LORA_RANK = 512
KV_LORA_RANK = 256
ROPE_THETA = 10000.0
NORM_EPS = 1e-6
NEG_INF = -1e30
LOG2E = 1.4426950408889634

LANES = 128
B_QK_PAD = 256
KPE_PAD = LANES

OFF_AZ = 3 * A_WIDTH
OFF_CQ = OFF_AZ + A_WIDTH
OFF_CKV = OFF_CQ + Q_LORA_RANK

ROW_TILE = 512
BAND_Q = 128
MLA_Q = 256
MLA_HEAD_GROUP = 2
VMEM_LIMIT = 56 * 1024 * 1024


def _silu(v):
    return v * (1.0 / (1.0 + jnp.exp(-v)))


def _rms(v, g):
    return v * lax.rsqrt(jnp.mean(v * v, axis=-1, keepdims=True) + NORM_EPS) * g


def _dot_nt(a, b):
    return lax.dot_general(a, b, (((1,), (1,)), ((), ())), preferred_element_type=F32)


def _ada_kernel(c_ref, w_ref, b_ref, o_ref):
    s = _silu(c_ref[...]).astype(BF16)
    o_ref[...] = jnp.dot(s, w_ref[...].astype(BF16), preferred_element_type=F32) + b_ref[...]


def _ada(c, w_ada, b_ada):
    bsz, d = c.shape
    n = w_ada.shape[1]
    tn = 512
    return pl.pallas_call(
        _ada_kernel,
        grid=(n // tn,),
        in_specs=[
            pl.BlockSpec((bsz, d), lambda j: (0, 0)),
            pl.BlockSpec((d, tn), lambda j: (0, j)),
            pl.BlockSpec((1, tn), lambda j: (0, j)),
        ],
        out_specs=pl.BlockSpec((bsz, tn), lambda j: (0, j)),
        out_shape=jax.ShapeDtypeStruct((bsz, n), F32),
        compiler_params=pltpu.CompilerParams(dimension_semantics=("arbitrary",)),
        name="ada_mod",
    )(c, w_ada, b_ada.reshape(1, n))


def _cast_w_in_kernel(w_ref, qkv_ref, mid_ref, bz_ref, kpe_ref):
    kpe0 = OFF_CKV + KV_LORA_RANK
    qkv_ref[...] = w_ref[:OFF_AZ, :].astype(BF16)
    mid_ref[...] = w_ref[OFF_AZ:kpe0, :].astype(BF16)
    bz_ref[...] = w_ref[kpe0 + B_ROPE_DIM:, :].astype(BF16)
    kpe_ref[:B_ROPE_DIM, :] = w_ref[kpe0:kpe0 + B_ROPE_DIM, :].astype(BF16)
    kpe_ref[B_ROPE_DIM:, :] = jnp.zeros((KPE_PAD - B_ROPE_DIM, w_ref.shape[1]), BF16)


def _cast_w_in(w_in_t):
    n, d = w_in_t.shape
    tc = 256
    heights = (OFF_AZ, OFF_CKV + KV_LORA_RANK - OFF_AZ, B_WIDTH, KPE_PAD)
    return pl.pallas_call(
        _cast_w_in_kernel,
        grid=(d // tc,),
        in_specs=[pl.BlockSpec((n, tc), lambda j: (0, j))],
        out_specs=[pl.BlockSpec((hgt, tc), lambda j: (0, j)) for hgt in heights],
        out_shape=[jax.ShapeDtypeStruct((hgt, d), BF16) for hgt in heights],
        compiler_params=pltpu.CompilerParams(dimension_semantics=("arbitrary",),
                                             vmem_limit_bytes=VMEM_LIMIT),
        name="cast_w_in",
    )(w_in_t)


def _qkv_kernel(x_ref, mod_ref, g_ref, w_ref, h_ref, s1_ref, s4_ref, s16_ref, acc_ref, acc4_ref):
    tm = x_ref.shape[0]
    d4 = DILATED_PATTERNS[1][1]
    x = x_ref[...]
    gain = g_ref[...] * (1.0 + mod_ref[0, 1:2, :])
    inv = lax.rsqrt(jnp.mean(x * x, axis=-1, keepdims=True) + NORM_EPS)
    h_ref[...] = (x * inv * gain + mod_ref[0, 0:1, :]).astype(BF16)

    heads_per = acc_ref.shape[0]
    chunk = heads_per * A_HEAD_DIM
    q_scale = A_HEAD_DIM ** -0.5 * LOG2E
    for t in range(3):
        for c0 in range(0, A_WIDTH, chunk):
            acc = _dot_nt(h_ref[...], w_ref[t * A_WIDTH + c0:t * A_WIDTH + c0 + chunk, :])
            if t == 0:
                acc = acc * q_scale
            for k in range(heads_per):
                head = c0 // A_HEAD_DIM + k
                a = acc[:, k * A_HEAD_DIM:(k + 1) * A_HEAD_DIM]
                s1_ref[t, 0, head, :, :] = a.astype(BF16)
                acc_ref[k, :, :] = a
            for k in range(heads_per):
                head = c0 // A_HEAD_DIM + k
                for r4 in range(d4):
                    g4 = acc_ref[k, pl.ds(r4, tm // d4, stride=d4), :]
                    s4_ref[t, 0, head, :, r4 * LANES:(r4 + 1) * LANES] = g4.astype(BF16)
                    acc4_ref[k, r4, :, :] = g4
                for r4 in range(d4):
                    for a4 in range(d4):
                        r = r4 + d4 * a4
                        s16_ref[t, 0, head, :, r * LANES:(r + 1) * LANES] = (
                            acc4_ref[k, r4, pl.ds(a4, tm // (d4 * d4), stride=d4), :].astype(BF16))


def _qkv_proj(x2, mod3, g_pre, w_qkv, bsz, seq):
    n_tok, d = x2.shape
    tm = ROW_TILE
    per_b = seq // tm
    hd = A_HEAD_DIM
    d4, d16 = DILATED_PATTERNS[1][1], DILATED_PATTERNS[2][1]
    head_major = lambda i: (0, i // per_b, 0, i % per_b, 0)
    return pl.pallas_call(
        _qkv_kernel,
        grid=(n_tok // tm,),
        in_specs=[
            pl.BlockSpec((tm, d), lambda i: (i, 0)),
            pl.BlockSpec((1, 3, d), lambda i: (i // per_b, 0, 0)),
            pl.BlockSpec((1, d), lambda i: (0, 0)),
            pl.BlockSpec(w_qkv.shape, lambda i: (0, 0), pipeline_mode=pl.Buffered(1)),
        ],
        out_specs=[
            pl.BlockSpec((tm, d), lambda i: (i, 0)),
            pl.BlockSpec((3, 1, A_HEADS, tm, hd), head_major),
            pl.BlockSpec((3, 1, A_HEADS, tm // d4, d4 * hd), head_major),
            pl.BlockSpec((3, 1, A_HEADS, tm // d16, d16 * hd), head_major),
        ],
        out_shape=[
            jax.ShapeDtypeStruct((n_tok, d), BF16),
            jax.ShapeDtypeStruct((3, bsz, A_HEADS, seq, hd), BF16),
            jax.ShapeDtypeStruct((3, bsz, A_HEADS, seq // d4, d4 * hd), BF16),
            jax.ShapeDtypeStruct((3, bsz, A_HEADS, seq // d16, d16 * hd), BF16),
        ],
        scratch_shapes=[pltpu.VMEM((4, tm, hd), F32), pltpu.VMEM((4, d4, tm // d4, hd), F32)],
        compiler_params=pltpu.CompilerParams(dimension_semantics=("arbitrary",),
                                             vmem_limit_bytes=VMEM_LIMIT),
        name="qkv_proj",
    )(x2, mod3, g_pre.reshape(1, d), w_qkv)


def _band_tables():
    bq, win = BAND_Q, 2 * BAND_Q
    radius = DILATED_PATTERNS[0][0] // 2 // DILATED_PATTERNS[0][1]
    rows = np.arange(bq)[:, None]
    cols = np.arange(win)[None, :]
    tabs = []
    for k_minus_q in (0, -radius, -bq, 0):
        dist = np.abs(cols + k_minus_q - rows)
        tabs.append(np.where(dist <= radius, -LOG2E * dist, NEG_INF))
    tabs[3][:, bq:] = NEG_INF
    return np.stack(tabs).astype(np.float32)


def _band_block(q, k_win, v_win, bias):
    s = _dot_nt(q, k_win) + bias
    m = jnp.max(s, axis=-1, keepdims=True)
    p = jnp.exp2(s - m).astype(BF16)
    v_ext = jnp.concatenate([v_win, jnp.ones_like(v_win)], axis=1)
    ol = jnp.dot(p, v_ext, preferred_element_type=F32)
    return ol[:, :A_HEAD_DIM], m, ol[:, A_HEAD_DIM:]


def _dil_kernel(slope_ref, tab_ref, s1_ref, s4_ref, s16_ref, zg_ref, y_ref, acc_ref, m_ref, l_ref):
    seq = s1_ref.shape[3]
    slope = slope_ref[pl.program_id(1)]
    bq = BAND_Q
    win = 2 * bq
    views = (s1_ref, s4_ref, s16_ref)

    def blocks_of(p_idx):
        window, dil = DILATED_PATTERNS[p_idx]
        radius = window // 2 // dil
        length = seq // dil
        ref = views[p_idx]
        pen = slope * float(dil)
        n_blk = length // bq
        bias = ([pen * tab_ref[3, :, :bq]] if n_blk == 1 else
                [pen * tab_ref[kind] for kind in range(3)])
        for r in range(dil):
            cols = slice(r * LANES, (r + 1) * LANES)
            if n_blk == 1:
                yield 0, r, _band_block(ref[0, 0, 0, :, cols], ref[1, 0, 0, :, cols],
                                        ref[2, 0, 0, :, cols], bias[0])
                continue
            for qi in range(n_blk):
                kind = 0 if qi == 0 else (2 if qi == n_blk - 1 else 1)
                q_lo = qi * bq
                k_lo = min(max(q_lo - radius, 0), length - win)
                yield q_lo, r, _band_block(ref[0, 0, 0, pl.ds(q_lo, bq), cols],
                                           ref[1, 0, 0, pl.ds(k_lo, win), cols],
                                           ref[2, 0, 0, pl.ds(k_lo, win), cols], bias[kind])

    for p_idx in range(1, len(DILATED_PATTERNS)):
        dil = DILATED_PATTERNS[p_idx][1]
        for q_lo, r, (acc, m, l) in blocks_of(p_idx):
            rows = pl.ds(q_lo * dil + r, bq, stride=dil)
            acc_ref[p_idx - 1, rows, :] = acc
            m_ref[p_idx - 1, rows, :] = jnp.broadcast_to(m, acc.shape)
            l_ref[p_idx - 1, rows, :] = l

    for q_lo, _, (acc0, m0, l0) in blocks_of(0):
        rs = pl.ds(q_lo, bq)
        m1, m2 = m_ref[0, rs, :], m_ref[1, rs, :]
        m = jnp.maximum(jnp.maximum(m0, m1), m2)
        e0, e1, e2 = jnp.exp2(m0 - m), jnp.exp2(m1 - m), jnp.exp2(m2 - m)
        num = e0 * acc0 + e1 * acc_ref[0, rs, :] + e2 * acc_ref[1, rs, :]
        den = e0 * l0 + e1 * l_ref[0, rs, :] + e2 * l_ref[1, rs, :]
        y_ref[0, rs, :] = (num / den * zg_ref[0, rs, :].astype(F32)).astype(BF16)


def _dilated(views, zg3, slopes):
    _, bsz, heads, seq, hd = views[0].shape
    assert DILATED_PATTERNS[0][1] == 1 and len(views) == len(DILATED_PATTERNS)
    tables = _band_tables()
    specs = [pl.BlockSpec((3, 1, 1) + v.shape[3:], lambda b, h: (0, b, h, 0, 0)) for v in views]
    tok_spec = pl.BlockSpec((1, seq, hd), lambda b, h: (b, 0, h))
    n_strided = len(DILATED_PATTERNS) - 1
    return pl.pallas_call(
        _dil_kernel,
        grid=(bsz, heads),
        in_specs=[pl.BlockSpec(memory_space=pltpu.SMEM),
                  pl.BlockSpec(tables.shape, lambda b, h: (0, 0, 0))] + specs + [tok_spec],
        out_specs=tok_spec,
        out_shape=jax.ShapeDtypeStruct((bsz, seq, heads * hd), BF16),
        scratch_shapes=[pltpu.VMEM((n_strided, seq, hd), F32) for _ in range(3)],
        compiler_params=pltpu.CompilerParams(dimension_semantics=("arbitrary", "arbitrary"),
                                             vmem_limit_bytes=VMEM_LIMIT),
        name="dilated_mix",
    )(slopes, jnp.asarray(tables), *views, zg3)


def _rope(pe, cc, sa, sb):
    return pe * cc + pltpu.roll(pe, 96, 1) * sa + pltpu.roll(pe, 32, 1) * sb


def _mla_proj_kernel(h_ref, wmid_ref, wbz_ref, wkpe_ref, gq_ref, gkv_ref, wq_ref, wkv_ref,
                     cc_ref, sa_ref, sb_ref, zg_ref, bzg_ref, qf_ref, kf_ref, v_ref):
    def proj(w_ref, off, width):
        return _dot_nt(h_ref[...], w_ref[off:off + width, :])

    chunk = 4 * LANES
    for c0 in range(0, A_WIDTH, chunk):
        zg_ref[:, c0:c0 + chunk] = _silu(proj(wmid_ref, c0, chunk)).astype(BF16)
    for c0 in range(0, B_WIDTH, chunk):
        bzg_ref[:, c0:c0 + chunk] = _silu(proj(wbz_ref, c0, chunk)).astype(BF16)

    cc, sa, sb = cc_ref[...], sa_ref[...], sb_ref[...]
    q_scale = (B_NOPE_DIM + B_ROPE_DIM) ** -0.5 * LOG2E
    cqn = _rms(proj(wmid_ref, OFF_CQ - OFF_AZ, Q_LORA_RANK), gq_ref[...]).astype(BF16)
    for h in range(B_HEADS):
        q = jnp.dot(cqn, wq_ref[:, h * B_QK_PAD:(h + 1) * B_QK_PAD], preferred_element_type=F32)
        qf_ref[0, h, :, :B_NOPE_DIM] = (q[:, :B_NOPE_DIM] * q_scale).astype(BF16)
        qf_ref[0, h, :, B_NOPE_DIM:] = (_rope(q[:, B_NOPE_DIM:], cc, sa, sb) * q_scale).astype(BF16)
    k_rot = _rope(proj(wkpe_ref, 0, KPE_PAD), cc, sa, sb).astype(BF16)
    ckvn = _rms(proj(wmid_ref, OFF_CKV - OFF_AZ, KV_LORA_RANK), gkv_ref[...]).astype(BF16)
    kv_w = B_NOPE_DIM + B_V_DIM
    for h in range(B_HEADS):
        kv = jnp.dot(ckvn, wkv_ref[:, h * kv_w:(h + 1) * kv_w], preferred_element_type=F32)
        kf_ref[0, h, :, :B_NOPE_DIM] = kv[:, :B_NOPE_DIM].astype(BF16)
        kf_ref[0, h, :, B_NOPE_DIM:] = k_rot
        v_ref[0, h, :, :] = kv[:, B_NOPE_DIM:].astype(BF16)


def _mla_proj(h, w_mid, w_bz, w_kpe, g_q, g_kv, wq_p, wkv, cc, sa, sb, bsz, seq):
    n_tok, d = h.shape
    tm = ROW_TILE
    per_b = seq // tm
    row = lambda i: (i, 0)
    full = lambda i: (0, 0)
    pos = lambda i: (i % per_b, 0)
    head_major = lambda i: (i // per_b, 0, i % per_b, 0)
    resident = lambda w: pl.BlockSpec(w.shape, full, pipeline_mode=pl.Buffered(1))
    return pl.pallas_call(
        _mla_proj_kernel,
        grid=(n_tok // tm,),
        in_specs=[
            pl.BlockSpec((tm, d), row),
            resident(w_mid), resident(w_bz), resident(w_kpe),
            pl.BlockSpec((1, Q_LORA_RANK), full),
            pl.BlockSpec((1, KV_LORA_RANK), full),
            resident(wq_p), resident(wkv),
            pl.BlockSpec((tm, LANES), pos),
            pl.BlockSpec((tm, LANES), pos),
            pl.BlockSpec((tm, LANES), pos),
        ],
        out_specs=[
            pl.BlockSpec((tm, A_WIDTH), row),
            pl.BlockSpec((tm, B_WIDTH), row),
            pl.BlockSpec((1, B_HEADS, tm, B_QK_PAD), head_major),
            pl.BlockSpec((1, B_HEADS, tm, B_QK_PAD), head_major),
            pl.BlockSpec((1, B_HEADS, tm, B_V_DIM), head_major),
        ],
        out_shape=[
            jax.ShapeDtypeStruct((n_tok, A_WIDTH), BF16),
            jax.ShapeDtypeStruct((n_tok, B_WIDTH), BF16),
            jax.ShapeDtypeStruct((bsz, B_HEADS, seq, B_QK_PAD), BF16),
            jax.ShapeDtypeStruct((bsz, B_HEADS, seq, B_QK_PAD), BF16),
            jax.ShapeDtypeStruct((bsz, B_HEADS, seq, B_V_DIM), BF16),
        ],
        compiler_params=pltpu.CompilerParams(dimension_semantics=("arbitrary",),
                                             vmem_limit_bytes=VMEM_LIMIT),
        name="mla_proj",
    )(h, w_mid, w_bz, w_kpe, g_q.reshape(1, -1), g_kv.reshape(1, -1), wq_p, wkv, cc, sa, sb)


def _mla_attn_kernel(q_ref, k_ref, v_ref, zg_ref, y_ref, vext_ref, s_ref):
    heads = q_ref.shape[1]
    seq = k_ref.shape[2]
    n_blk = seq // MLA_Q
    for hh in range(heads):
        vext_ref[hh, :, :B_V_DIM] = v_ref[0, hh, :, :]
        vext_ref[hh, :, B_V_DIM:] = jnp.ones((seq, B_V_DIM), BF16)

    def rows_of(blk):
        start = blk * MLA_Q
        return pl.ds(start if isinstance(blk, int) else pl.multiple_of(start, MLA_Q), MLA_Q)

    def scores(hh, blk, slot):
        s_ref[slot] = _dot_nt(q_ref[0, hh, rows_of(blk), :], k_ref[0, hh, :, :])

    def softmax_pv(hh, blk, slot):
        s = s_ref[slot]
        m = jnp.max(s, axis=-1, keepdims=True)
        p = jnp.exp2(s - m).astype(BF16)
        ol = jnp.dot(p, vext_ref[hh], preferred_element_type=F32)
        o = ol[:, :B_V_DIM] / ol[:, B_V_DIM:]
        rows = rows_of(blk)
        cols = slice(hh * B_V_DIM, (hh + 1) * B_V_DIM)
        y_ref[0, rows, cols] = (o * zg_ref[0, rows, cols].astype(F32)).astype(BF16)

    items = [(hh, blk) for hh in range(heads) for blk in range(n_blk)]
    scores(*items[0], 0)
    for n, item in enumerate(items):
        if n + 1 < len(items):
            scores(*items[n + 1], (n + 1) % 2)
        softmax_pv(*item, n % 2)


def _mla_attn(qf, kf, v, bzg3):
    bsz, heads, seq, _ = qf.shape
    grp = MLA_HEAD_GROUP
    head_blk = lambda b, g: (b, g, 0, 0)
    tok_spec = pl.BlockSpec((1, seq, grp * B_V_DIM), lambda b, g: (b, 0, g))
    return pl.pallas_call(
        _mla_attn_kernel,
        grid=(bsz, heads // grp),
        in_specs=[
            pl.BlockSpec((1, grp, seq, B_QK_PAD), head_blk),
            pl.BlockSpec((1, grp, seq, B_QK_PAD), head_blk),
            pl.BlockSpec((1, grp, seq, B_V_DIM), head_blk),
            tok_spec,
        ],
        out_specs=tok_spec,
        out_shape=jax.ShapeDtypeStruct((bsz, seq, heads * B_V_DIM), BF16),
        scratch_shapes=[pltpu.VMEM((grp, seq, 2 * B_V_DIM), BF16), pltpu.VMEM((2, MLA_Q, seq), F32)],
        compiler_params=pltpu.CompilerParams(dimension_semantics=("arbitrary", "arbitrary"),
                                             vmem_limit_bytes=VMEM_LIMIT),
        name="mla_attn",
    )(qf, kf, v, bzg3)


def _out_kernel(ya_ref, yb_ref, w_ref, x_ref, mod_ref, g_ref, o_ref, y_ref):
    tm, d = y_ref.shape
    chunk = 4 * LANES
    ssq = jnp.zeros((tm, LANES), F32)
    for c0 in range(0, d, chunk):
        y = (jnp.dot(ya_ref[...], w_ref[:A_WIDTH, c0:c0 + chunk], preferred_element_type=F32)
             + jnp.dot(yb_ref[...], w_ref[A_WIDTH:, c0:c0 + chunk], preferred_element_type=F32))
        y_ref[:, c0:c0 + chunk] = y
        for t0 in range(0, chunk, LANES):
            ssq = ssq + y[:, t0:t0 + LANES] * y[:, t0:t0 + LANES]
    inv = lax.rsqrt(jnp.sum(ssq, axis=-1, keepdims=True) * (1.0 / d) + NORM_EPS)
    gain = mod_ref[0, 2:3, :] * g_ref[...]
    o_ref[...] = x_ref[...] + y_ref[...] * inv * gain


def _out_proj(ya, yb, w_out, x2, mod3, g_post, seq):
    n_tok, d = x2.shape
    tm = ROW_TILE
    per_b = seq // tm
    row = lambda i: (i, 0)
    return pl.pallas_call(
        _out_kernel,
        grid=(n_tok // tm,),
        in_specs=[
            pl.BlockSpec((tm, A_WIDTH), row),
            pl.BlockSpec((tm, B_WIDTH), row),
            pl.BlockSpec(w_out.shape, lambda i: (0, 0), pipeline_mode=pl.Buffered(1)),
            pl.BlockSpec((tm, d), row),
            pl.BlockSpec((1, 3, d), lambda i: (i // per_b, 0, 0)),
            pl.BlockSpec((1, d), lambda i: (0, 0)),
        ],
        out_specs=pl.BlockSpec((tm, d), row),
        out_shape=jax.ShapeDtypeStruct((n_tok, d), F32),
        scratch_shapes=[pltpu.VMEM((tm, d), F32)],
        compiler_params=pltpu.CompilerParams(dimension_semantics=("arbitrary",),
                                             vmem_limit_bytes=VMEM_LIMIT),
        name="out_proj",
    )(ya, yb, w_out, x2, mod3, g_post.reshape(1, d))


def _pad_w_uq(w_uq):
    r = w_uq.shape[0]
    w = w_uq.reshape(r, B_HEADS, B_NOPE_DIM + B_ROPE_DIM)
    pad = jnp.zeros((r, B_HEADS, B_QK_PAD - B_NOPE_DIM - B_ROPE_DIM), w_uq.dtype)
    return jnp.concatenate([w, pad], axis=2).reshape(r, B_HEADS * B_QK_PAD).astype(BF16)


def _rope_tables(seq):
    half = B_ROPE_DIM // 2
    pos = jnp.arange(seq, dtype=F32)
    inv_freq = jnp.power(ROPE_THETA, -jnp.arange(half, dtype=F32) / half)
    ang = pos[:, None] * inv_freq[None, :]
    cos, sin = jnp.cos(ang), jnp.sin(ang)
    z = jnp.zeros_like(cos)
    cc = jnp.concatenate([cos, cos, z, z], axis=1)
    sa = jnp.concatenate([-sin, z, z, z], axis=1)
    sb = jnp.concatenate([z, sin, z, z], axis=1)
    return cc, sa, sb


def _layer(x, c, w_ada, b_ada, g_pre, w_in, g_q_lora, w_uq, g_kv_lora, w_ukv, w_out, g_post):
    bsz, seq, d = x.shape
    x2 = x.reshape(bsz * seq, d)
    mod3 = _ada(c, w_ada, b_ada).reshape(bsz, 3, d)
    w_qkv, w_mid, w_bz, w_kpe = _cast_w_in(w_in.T)
    h, s1, s4, s16 = _qkv_proj(x2, mod3, g_pre, w_qkv, bsz, seq)
    cc, sa, sb = _rope_tables(seq)
    zg, bzg, qf, kf, v = _mla_proj(h, w_mid, w_bz, w_kpe, g_q_lora, g_kv_lora, _pad_w_uq(w_uq),
                                   w_ukv.astype(BF16), cc, sa, sb, bsz, seq)
    slopes = jnp.exp2(-8.0 * jnp.arange(1, A_HEADS + 1, dtype=F32) / A_HEADS)
    ya = _dilated((s1, s4, s16), zg.reshape(bsz, seq, A_WIDTH), slopes)
    yb = _mla_attn(qf, kf, v, bzg.reshape(bsz, seq, B_WIDTH))
    out = _out_proj(ya.reshape(bsz * seq, A_WIDTH), yb.reshape(bsz * seq, B_WIDTH),
                    w_out.astype(BF16), x2, mod3, g_post, seq)
    return out.reshape(bsz, seq, d)


def kernel(x, c, w_ada, b_ada, g_pre, w_in, g_q_lora, w_uq, g_kv_lora, w_ukv, w_out, g_post):
    for layer in range(w_ada.shape[0]):
        x = _layer(x, c, w_ada[layer], b_ada[layer], g_pre[layer], w_in[layer], g_q_lora[layer],
                   w_uq[layer], g_kv_lora[layer], w_ukv[layer], w_out[layer], g_post[layer])
    return x
```

```python
import jax
import jax.numpy as jnp
import numpy as np
from jax import lax
from jax.experimental import pallas as pl
from jax.experimental.pallas import tpu as pltpu

F32 = jnp.float32
BF16 = jnp.bfloat16

D_MODEL = 2048
A_HEADS = 8
A_HEAD_DIM = 128
A_WIDTH = A_HEADS * A_HEAD_DIM
DILATED_PATTERNS = ((128, 1), (512, 4), (2048, 16))
B_HEADS = 8
B_V_DIM = 128
B_WIDTH = B_HEADS * B_V_DIM
B_NOPE_DIM = 128
B_ROPE_DIM = 64
Q_LORA_RANK = 512
KV_LORA_RANK = 256
ROPE_THETA = 10000.0
NORM_EPS = 1e-6
NEG_INF = -1e30
LOG2E = 1.4426950408889634

LANES = 128
B_QK_PAD = 256
KPE_PAD = LANES

OFF_AZ = 3 * A_WIDTH
OFF_CQ = OFF_AZ + A_WIDTH
OFF_CKV = OFF_CQ + Q_LORA_RANK

ROW_TILE = 512
BAND_Q = 128
BAND_HEAD_GROUP = 2
MLA_Q = 256
MLA_HEAD_GROUP = 2
VMEM_LIMIT = 56 * 1024 * 1024


def _silu(v):
    return v * (1.0 / (1.0 + jnp.exp(-v)))


def _rms(v, g):
    return v * lax.rsqrt(jnp.mean(v * v, axis=-1, keepdims=True) + NORM_EPS) * g


def _dot_nt(a, b):
    return lax.dot_general(a, b, (((1,), (1,)), ((), ())), preferred_element_type=F32)


def _ada_kernel(c_ref, w_ref, b_ref, o_ref):
    s = _silu(c_ref[...]).astype(BF16)
    o_ref[...] = jnp.dot(s, w_ref[...].astype(BF16), preferred_element_type=F32) + b_ref[...]


def _ada(c, w_ada, b_ada):
    bsz, d = c.shape
    n = w_ada.shape[1]
    tn = 512
    return pl.pallas_call(
        _ada_kernel,
        grid=(n // tn,),
        in_specs=[
            pl.BlockSpec((bsz, d), lambda j: (0, 0)),
            pl.BlockSpec((d, tn), lambda j: (0, j)),
            pl.BlockSpec((1, tn), lambda j: (0, j)),
        ],
        out_specs=pl.BlockSpec((bsz, tn), lambda j: (0, j)),
        out_shape=jax.ShapeDtypeStruct((bsz, n), F32),
        compiler_params=pltpu.CompilerParams(dimension_semantics=("arbitrary",)),
        name="ada_mod",
    )(c, w_ada, b_ada.reshape(1, n))


def _cast_w_in_kernel(w_ref, qkv_ref, mid_ref, bz_ref, kpe_ref):
    kpe0 = OFF_CKV + KV_LORA_RANK
    qkv_ref[...] = w_ref[:OFF_AZ, :].astype(BF16)
    mid_ref[...] = w_ref[OFF_AZ:kpe0, :].astype(BF16)
    bz_ref[...] = w_ref[kpe0 + B_ROPE_DIM:, :].astype(BF16)
    kpe_ref[:B_ROPE_DIM, :] = w_ref[kpe0:kpe0 + B_ROPE_DIM, :].astype(BF16)
    kpe_ref[B_ROPE_DIM:, :] = jnp.zeros((KPE_PAD - B_ROPE_DIM, w_ref.shape[1]), BF16)


def _cast_w_in(w_in_t):
    n, d = w_in_t.shape
    tc = 256
    heights = (OFF_AZ, OFF_CKV + KV_LORA_RANK - OFF_AZ, B_WIDTH, KPE_PAD)
    return pl.pallas_call(
        _cast_w_in_kernel,
        grid=(d // tc,),
        in_specs=[pl.BlockSpec((n, tc), lambda j: (0, j))],
        out_specs=[pl.BlockSpec((hgt, tc), lambda j: (0, j)) for hgt in heights],
        out_shape=[jax.ShapeDtypeStruct((hgt, d), BF16) for hgt in heights],
        compiler_params=pltpu.CompilerParams(dimension_semantics=("arbitrary",),
                                             vmem_limit_bytes=VMEM_LIMIT),
        name="cast_w_in",
    )(w_in_t)


def _qkv_kernel(x_ref, mod_ref, g_ref, w_ref, h_ref, s1_ref, s4_ref, s16_ref, acc_ref, acc4_ref):
    tm = x_ref.shape[0]
    d4 = DILATED_PATTERNS[1][1]
    x = x_ref[...]
    gain = g_ref[...] * (1.0 + mod_ref[0, 1:2, :])
    inv = lax.rsqrt(jnp.mean(x * x, axis=-1, keepdims=True) + NORM_EPS)
    h_ref[...] = (x * inv * gain + mod_ref[0, 0:1, :]).astype(BF16)

    heads_per = acc_ref.shape[0]
    chunk = heads_per * A_HEAD_DIM
    q_scale = A_HEAD_DIM ** -0.5 * LOG2E
    for t in range(3):
        for c0 in range(0, A_WIDTH, chunk):
            acc = _dot_nt(h_ref[...], w_ref[t * A_WIDTH + c0:t * A_WIDTH + c0 + chunk, :])
            if t == 0:
                acc = acc * q_scale
            for k in range(heads_per):
                head = c0 // A_HEAD_DIM + k
                a = acc[:, k * A_HEAD_DIM:(k + 1) * A_HEAD_DIM]
                s1_ref[t, 0, head, :, :] = a.astype(BF16)
                acc_ref[k, :, :] = a
            for k in range(heads_per):
                head = c0 // A_HEAD_DIM + k
                for r4 in range(d4):
                    g4 = acc_ref[k, pl.ds(r4, tm // d4, stride=d4), :]
                    s4_ref[t, 0, head, :, r4 * LANES:(r4 + 1) * LANES] = g4.astype(BF16)
                    acc4_ref[k, r4, :, :] = g4
                for r4 in range(d4):
                    for a4 in range(d4):
                        r = r4 + d4 * a4
                        s16_ref[t, 0, head, :, r * LANES:(r + 1) * LANES] = (
                            acc4_ref[k, r4, pl.ds(a4, tm // (d4 * d4), stride=d4), :].astype(BF16))


def _qkv_proj(x2, mod3, g_pre, w_qkv, bsz, seq):
    n_tok, d = x2.shape
    tm = ROW_TILE
    per_b = seq // tm
    hd = A_HEAD_DIM
    d4, d16 = DILATED_PATTERNS[1][1], DILATED_PATTERNS[2][1]
    head_major = lambda i: (0, i // per_b, 0, i % per_b, 0)
    return pl.pallas_call(
        _qkv_kernel,
        grid=(n_tok // tm,),
        in_specs=[
            pl.BlockSpec((tm, d), lambda i: (i, 0)),
            pl.BlockSpec((1, 3, d), lambda i: (i // per_b, 0, 0)),
            pl.BlockSpec((1, d), lambda i: (0, 0)),
            pl.BlockSpec(w_qkv.shape, lambda i: (0, 0), pipeline_mode=pl.Buffered(1)),
        ],
        out_specs=[
            pl.BlockSpec((tm, d), lambda i: (i, 0)),
            pl.BlockSpec((3, 1, A_HEADS, tm, hd), head_major),
            pl.BlockSpec((3, 1, A_HEADS, tm // d4, d4 * hd), head_major),
            pl.BlockSpec((3, 1, A_HEADS, tm // d16, d16 * hd), head_major),
        ],
        out_shape=[
            jax.ShapeDtypeStruct((n_tok, d), BF16),
            jax.ShapeDtypeStruct((3, bsz, A_HEADS, seq, hd), BF16),
            jax.ShapeDtypeStruct((3, bsz, A_HEADS, seq // d4, d4 * hd), BF16),
            jax.ShapeDtypeStruct((3, bsz, A_HEADS, seq // d16, d16 * hd), BF16),
        ],
        scratch_shapes=[pltpu.VMEM((4, tm, hd), F32), pltpu.VMEM((4, d4, tm // d4, hd), F32)],
        compiler_params=pltpu.CompilerParams(dimension_semantics=("arbitrary",),
                                             vmem_limit_bytes=VMEM_LIMIT),
        name="qkv_proj",
    )(x2, mod3, g_pre.reshape(1, d), w_qkv)


def _band_tables():
    bq, win = BAND_Q, 2 * BAND_Q
    radius = DILATED_PATTERNS[0][0] // 2 // DILATED_PATTERNS[0][1]
    rows = np.arange(bq)[:, None]
    cols = np.arange(win)[None, :]
    tabs = []
    for k_minus_q in (0, -radius, -bq, 0):
        dist = np.abs(cols + k_minus_q - rows)
        tabs.append(np.where(dist <= radius, -LOG2E * dist, NEG_INF))
    tabs[3][:, bq:] = NEG_INF
    return np.stack(tabs).astype(np.float32)


def _band_block(q, k_win, v_win, bias):
    s = _dot_nt(q, k_win) + bias
    m = jnp.max(s, axis=-1, keepdims=True)
    p = jnp.exp2(s - m).astype(BF16)
    v_ext = jnp.concatenate([v_win, jnp.ones_like(v_win)], axis=1)
    ol = jnp.dot(p, v_ext, preferred_element_type=F32)
    return ol[:, :A_HEAD_DIM], m, ol[:, A_HEAD_DIM:]


def _dil_kernel(slope_ref, tab_ref, s1_ref, s4_ref, s16_ref, zg_ref, y_ref,
                acc_ref, m_ref, l_ref, stage_ref):
    heads = s1_ref.shape[2]
    seq = s1_ref.shape[3]
    bq = BAND_Q
    win = 2 * bq
    views = (s1_ref, s4_ref, s16_ref)

    def blocks_of(hh, p_idx):
        window, dil = DILATED_PATTERNS[p_idx]
        radius = window // 2 // dil
        length = seq // dil
        ref = views[p_idx]
        pen = slope_ref[pl.program_id(1) * heads + hh] * float(dil)
        n_blk = length // bq
        bias = ([pen * tab_ref[3, :, :bq]] if n_blk == 1 else
                [pen * tab_ref[kind] for kind in range(3)])
        for r in sorted(range(dil), key=lambda r: (r % d4, r)):
            cols = slice(r * LANES, (r + 1) * LANES)
            if n_blk == 1:
                yield 0, r, _band_block(ref[0, 0, hh, :, cols], ref[1, 0, hh, :, cols],
                                        ref[2, 0, hh, :, cols], bias[0])
                continue
            for qi in range(n_blk):
                kind = 0 if qi == 0 else (2 if qi == n_blk - 1 else 1)
                q_lo = qi * bq
                k_lo = min(max(q_lo - radius, 0), length - win)
                yield q_lo, r, _band_block(ref[0, 0, hh, pl.ds(q_lo, bq), cols],
                                           ref[1, 0, hh, pl.ds(k_lo, win), cols],
                                           ref[2, 0, hh, pl.ds(k_lo, win), cols], bias[kind])

    stats = (acc_ref, m_ref, l_ref)
    d4 = DILATED_PATTERNS[1][1]

    def strided_phase(hh):
        for p_idx in range(1, len(DILATED_PATTERNS)):
            dil = DILATED_PATTERNS[p_idx][1]
            for q_lo, r, (acc, m, l) in blocks_of(hh, p_idx):
                vals = (acc, jnp.broadcast_to(m, acc.shape), l)
                if dil == d4:
                    rows = pl.ds(q_lo * dil + r, bq, stride=dil)
                    for ref, val in zip(stats, vals):
                        ref[hh, p_idx - 1, rows, :] = val
                else:
                    r4, a4 = r % d4, r // d4
                    for k, val in enumerate(vals):
                        stage_ref[hh, k, r4, pl.ds(a4, bq, stride=d4), :] = val
                    if a4 == d4 - 1:
                        for k, ref in enumerate(stats):
                            ref[hh, p_idx - 1, pl.ds(r4, seq // d4, stride=d4), :] = (
                                stage_ref[hh, k, r4, :, :])
                yield

    def mix_phase(hh):
        out_cols = slice(hh * A_HEAD_DIM, (hh + 1) * A_HEAD_DIM)
        for q_lo, _, (acc0, m0, l0) in blocks_of(hh, 0):
            rs = pl.ds(q_lo, bq)
            m1, m2 = m_ref[hh, 0, rs, :], m_ref[hh, 1, rs, :]
            m = jnp.maximum(jnp.maximum(m0, m1), m2)
            e0, e1, e2 = jnp.exp2(m0 - m), jnp.exp2(m1 - m), jnp.exp2(m2 - m)
            num = e0 * acc0 + e1 * acc_ref[hh, 0, rs, :] + e2 * acc_ref[hh, 1, rs, :]
            den = e0 * l0 + e1 * l_ref[hh, 0, rs, :] + e2 * l_ref[hh, 1, rs, :]
            y_ref[0, rs, out_cols] = (num / den * zg_ref[0, rs, out_cols].astype(F32)).astype(BF16)
            yield

    prev_mix = None
    for hh in range(heads):
        cur = strided_phase(hh)
        if prev_mix is None:
            for _ in cur:
                pass
        else:
            for _ in cur:
                next(prev_mix, None)
                next(cur, None)
            for _ in prev_mix:
                pass
        prev_mix = mix_phase(hh)
    for _ in prev_mix:
        pass


def _dilated(views, zg3, slopes):
    _, bsz, heads, seq, hd = views[0].shape
    assert DILATED_PATTERNS[0][1] == 1 and len(views) == len(DILATED_PATTERNS)
    grp = BAND_HEAD_GROUP
    d4 = DILATED_PATTERNS[1][1]
    assert [p[1] for p in DILATED_PATTERNS] == [1, d4, d4 * d4] and d4 == 4
    tables = _band_tables()
    specs = [pl.BlockSpec((3, 1, grp) + v.shape[3:], lambda b, g: (0, b, g, 0, 0)) for v in views]
    tok_spec = pl.BlockSpec((1, seq, grp * hd), lambda b, g: (b, 0, g))
    n_strided = len(DILATED_PATTERNS) - 1
    return pl.pallas_call(
        _dil_kernel,
        grid=(bsz, heads // grp),
        in_specs=[pl.BlockSpec(memory_space=pltpu.SMEM),
                  pl.BlockSpec(tables.shape, lambda b, g: (0, 0, 0))] + specs + [tok_spec],
        out_specs=tok_spec,
        out_shape=jax.ShapeDtypeStruct((bsz, seq, heads * hd), BF16),
        scratch_shapes=([pltpu.VMEM((grp, n_strided, seq, hd), F32) for _ in range(3)]
                        + [pltpu.VMEM((grp, 3, d4, seq // d4, hd), F32)]),
        compiler_params=pltpu.CompilerParams(dimension_semantics=("arbitrary", "arbitrary"),
                                             vmem_limit_bytes=VMEM_LIMIT),
        name="dilated_mix",
    )(slopes, jnp.asarray(tables), *views, zg3)


def _rope(pe, cc, sa, sb):
    return pe * cc + pltpu.roll(pe, 96, 1) * sa + pltpu.roll(pe, 32, 1) * sb


def _mla_proj_kernel(h_ref, wmid_ref, wbz_ref, wkpe_ref, gq_ref, gkv_ref, wq_ref, wkv_ref,
                     cc_ref, sa_ref, sb_ref, zg_ref, bzg_ref, qf_ref, kf_ref, v_ref):
    def proj(w_ref, off, width):
        return _dot_nt(h_ref[...], w_ref[off:off + width, :])

    chunk = 4 * LANES
    for c0 in range(0, A_WIDTH, chunk):
        zg_ref[:, c0:c0 + chunk] = _silu(proj(wmid_ref, c0, chunk)).astype(BF16)
    for c0 in range(0, B_WIDTH, chunk):
        bzg_ref[:, c0:c0 + chunk] = _silu(proj(wbz_ref, c0, chunk)).astype(BF16)

    cc, sa, sb = cc_ref[...], sa_ref[...], sb_ref[...]
    q_scale = (B_NOPE_DIM + B_ROPE_DIM) ** -0.5 * LOG2E
    cqn = _rms(proj(wmid_ref, OFF_CQ - OFF_AZ, Q_LORA_RANK), gq_ref[...]).astype(BF16)
    for h in range(B_HEADS):
        q = jnp.dot(cqn, wq_ref[:, h * B_QK_PAD:(h + 1) * B_QK_PAD], preferred_element_type=F32)
        qf_ref[0, h, :, :B_NOPE_DIM] = (q[:, :B_NOPE_DIM] * q_scale).astype(BF16)
        qf_ref[0, h, :, B_NOPE_DIM:] = (_rope(q[:, B_NOPE_DIM:], cc, sa, sb) * q_scale).astype(BF16)
    k_rot = _rope(proj(wkpe_ref, 0, KPE_PAD), cc, sa, sb).astype(BF16)
    ckvn = _rms(proj(wmid_ref, OFF_CKV - OFF_AZ, KV_LORA_RANK), gkv_ref[...]).astype(BF16)
    kv_w = B_NOPE_DIM + B_V_DIM
    for h in range(B_HEADS):
        kv = jnp.dot(ckvn, wkv_ref[:, h * kv_w:(h + 1) * kv_w], preferred_element_type=F32)
        kf_ref[0, h, :, :B_NOPE_DIM] = kv[:, :B_NOPE_DIM].astype(BF16)
        kf_ref[0, h, :, B_NOPE_DIM:] = k_rot
        v_ref[0, h, :, :] = kv[:, B_NOPE_DIM:].astype(BF16)


def _mla_proj(h, w_mid, w_bz, w_kpe, g_q, g_kv, wq_p, wkv, cc, sa, sb, bsz, seq):
    n_tok, d = h.shape
    tm = ROW_TILE
    per_b = seq // tm
    row = lambda i: (i, 0)
    full = lambda i: (0, 0)
    pos = lambda i: (i % per_b, 0)
    head_major = lambda i: (i // per_b, 0, i % per_b, 0)
    resident = lambda w: pl.BlockSpec(w.shape, full, pipeline_mode=pl.Buffered(1))
    return pl.pallas_call(
        _mla_proj_kernel,
        grid=(n_tok // tm,),
        in_specs=[
            pl.BlockSpec((tm, d), row),
            resident(w_mid), resident(w_bz), resident(w_kpe),
            pl.BlockSpec((1, Q_LORA_RANK), full),
            pl.BlockSpec((1, KV_LORA_RANK), full),
            resident(wq_p), resident(wkv),
            pl.BlockSpec((tm, LANES), pos),
            pl.BlockSpec((tm, LANES), pos),
            pl.BlockSpec((tm, LANES), pos),
        ],
        out_specs=[
            pl.BlockSpec((tm, A_WIDTH), row),
            pl.BlockSpec((tm, B_WIDTH), row),
            pl.BlockSpec((1, B_HEADS, tm, B_QK_PAD), head_major),
            pl.BlockSpec((1, B_HEADS, tm, B_QK_PAD), head_major),
            pl.BlockSpec((1, B_HEADS, tm, B_V_DIM), head_major),
        ],
        out_shape=[
            jax.ShapeDtypeStruct((n_tok, A_WIDTH), BF16),
            jax.ShapeDtypeStruct((n_tok, B_WIDTH), BF16),
            jax.ShapeDtypeStruct((bsz, B_HEADS, seq, B_QK_PAD), BF16),
            jax.ShapeDtypeStruct((bsz, B_HEADS, seq, B_QK_PAD), BF16),
            jax.ShapeDtypeStruct((bsz, B_HEADS, seq, B_V_DIM), BF16),
        ],
        compiler_params=pltpu.CompilerParams(dimension_semantics=("arbitrary",),
                                             vmem_limit_bytes=VMEM_LIMIT),
        name="mla_proj",
    )(h, w_mid, w_bz, w_kpe, g_q.reshape(1, -1), g_kv.reshape(1, -1), wq_p, wkv, cc, sa, sb)


def _mla_attn_kernel(q_ref, k_ref, v_ref, zg_ref, y_ref, vext_ref, s_ref):
    heads = q_ref.shape[1]
    seq = k_ref.shape[2]
    n_blk = seq // MLA_Q
    for hh in range(heads):
        vext_ref[hh, :, :B_V_DIM] = v_ref[0, hh, :, :]
        vext_ref[hh, :, B_V_DIM:] = jnp.ones((seq, B_V_DIM), BF16)

    def rows_of(blk):
        start = blk * MLA_Q
        return pl.ds(start if isinstance(blk, int) else pl.multiple_of(start, MLA_Q), MLA_Q)

    def scores(hh, blk, slot):
        s_ref[slot] = _dot_nt(q_ref[0, hh, rows_of(blk), :], k_ref[0, hh, :, :])

    def softmax_pv(hh, blk, slot):
        s = s_ref[slot]
        m = jnp.max(s, axis=-1, keepdims=True)
        p = jnp.exp2(s - m).astype(BF16)
        ol = jnp.dot(p, vext_ref[hh], preferred_element_type=F32)
        o = ol[:, :B_V_DIM] / ol[:, B_V_DIM:]
        rows = rows_of(blk)
        cols = slice(hh * B_V_DIM, (hh + 1) * B_V_DIM)
        y_ref[0, rows, cols] = (o * zg_ref[0, rows, cols].astype(F32)).astype(BF16)

    items = [(hh, blk) for hh in range(heads) for blk in range(n_blk)]
    scores(*items[0], 0)
    for n, item in enumerate(items):
        if n + 1 < len(items):
            scores(*items[n + 1], (n + 1) % 2)
        softmax_pv(*item, n % 2)


def _mla_attn(qf, kf, v, bzg3):
    bsz, heads, seq, _ = qf.shape
    grp = MLA_HEAD_GROUP
    head_blk = lambda b, g: (b, g, 0, 0)
    tok_spec = pl.BlockSpec((1, seq, grp * B_V_DIM), lambda b, g: (b, 0, g))
    return pl.pallas_call(
        _mla_attn_kernel,
        grid=(bsz, heads // grp),
        in_specs=[
            pl.BlockSpec((1, grp, seq, B_QK_PAD), head_blk),
            pl.BlockSpec((1, grp, seq, B_QK_PAD), head_blk),
            pl.BlockSpec((1, grp, seq, B_V_DIM), head_blk),
            tok_spec,
        ],
        out_specs=tok_spec,
        out_shape=jax.ShapeDtypeStruct((bsz, seq, heads * B_V_DIM), BF16),
        scratch_shapes=[pltpu.VMEM((grp, seq, 2 * B_V_DIM), BF16), pltpu.VMEM((2, MLA_Q, seq), F32)],
        compiler_params=pltpu.CompilerParams(dimension_semantics=("arbitrary", "arbitrary"),
                                             vmem_limit_bytes=VMEM_LIMIT),
        name="mla_attn",
    )(qf, kf, v, bzg3)


def _out_kernel(ya_ref, yb_ref, w_ref, x_ref, mod_ref, g_ref, o_ref, y_ref):
    tm, d = y_ref.shape
    chunk = 4 * LANES
    ssq = jnp.zeros((tm, LANES), F32)
    for c0 in range(0, d, chunk):
        y = (jnp.dot(ya_ref[...], w_ref[:A_WIDTH, c0:c0 + chunk], preferred_element_type=F32)
             + jnp.dot(yb_ref[...], w_ref[A_WIDTH:, c0:c0 + chunk], preferred_element_type=F32))
        y_ref[:, c0:c0 + chunk] = y
        for t0 in range(0, chunk, LANES):
            ssq = ssq + y[:, t0:t0 + LANES] * y[:, t0:t0 + LANES]
    inv = lax.rsqrt(jnp.sum(ssq, axis=-1, keepdims=True) * (1.0 / d) + NORM_EPS)
    gain = mod_ref[0, 2:3, :] * g_ref[...]
    o_ref[...] = x_ref[...] + y_ref[...] * inv * gain


def _out_proj(ya, yb, w_out, x2, mod3, g_post, seq):
    n_tok, d = x2.shape
    tm = ROW_TILE
    per_b = seq // tm
    row = lambda i: (i, 0)
    return pl.pallas_call(
        _out_kernel,
        grid=(n_tok // tm,),
        in_specs=[
            pl.BlockSpec((tm, A_WIDTH), row),
            pl.BlockSpec((tm, B_WIDTH), row),
            pl.BlockSpec(w_out.shape, lambda i: (0, 0), pipeline_mode=pl.Buffered(1)),
            pl.BlockSpec((tm, d), row),
            pl.BlockSpec((1, 3, d), lambda i: (i // per_b, 0, 0)),
            pl.BlockSpec((1, d), lambda i: (0, 0)),
        ],
        out_specs=pl.BlockSpec((tm, d), row),
        out_shape=jax.ShapeDtypeStruct((n_tok, d), F32),
        scratch_shapes=[pltpu.VMEM((tm, d), F32)],
        compiler_params=pltpu.CompilerParams(dimension_semantics=("arbitrary",),
                                             vmem_limit_bytes=VMEM_LIMIT),
        name="out_proj",
    )(ya, yb, w_out, x2, mod3, g_post.reshape(1, d))


def _pad_w_uq(w_uq):
    r = w_uq.shape[0]
    w = w_uq.reshape(r, B_HEADS, B_NOPE_DIM + B_ROPE_DIM)
    pad = jnp.zeros((r, B_HEADS, B_QK_PAD - B_NOPE_DIM - B_ROPE_DIM), w_uq.dtype)
    return jnp.concatenate([w, pad], axis=2).reshape(r, B_HEADS * B_QK_PAD).astype(BF16)


def _rope_tables(seq):
    half = B_ROPE_DIM // 2
    pos = jnp.arange(seq, dtype=F32)
    inv_freq = jnp.power(ROPE_THETA, -jnp.arange(half, dtype=F32) / half)
    ang = pos[:, None] * inv_freq[None, :]
    cos, sin = jnp.cos(ang), jnp.sin(ang)
    z = jnp.zeros_like(cos)
    cc = jnp.concatenate([cos, cos, z, z], axis=1)
    sa = jnp.concatenate([-sin, z, z, z], axis=1)
    sb = jnp.concatenate([z, sin, z, z], axis=1)
    return cc, sa, sb


def _layer(x, c, w_ada, b_ada, g_pre, w_in, g_q_lora, w_uq, g_kv_lora, w_ukv, w_out, g_post):
    bsz, seq, d = x.shape
    x2 = x.reshape(bsz * seq, d)
    mod3 = _ada(c, w_ada, b_ada).reshape(bsz, 3, d)
    w_qkv, w_mid, w_bz, w_kpe = _cast_w_in(w_in.T)
    h, s1, s4, s16 = _qkv_proj(x2, mod3, g_pre, w_qkv, bsz, seq)
    cc, sa, sb = _rope_tables(seq)
    zg, bzg, qf, kf, v = _mla_proj(h, w_mid, w_bz, w_kpe, g_q_lora, g_kv_lora, _pad_w_uq(w_uq),
                                   w_ukv.astype(BF16), cc, sa, sb, bsz, seq)
    slopes = jnp.exp2(-8.0 * jnp.arange(1, A_HEADS + 1, dtype=F32) / A_HEADS)
    ya = _dilated((s1, s4, s16), zg.reshape(bsz, seq, A_WIDTH), slopes)
    yb = _mla_attn(qf, kf, v, bzg.reshape(bsz, seq, B_WIDTH))
    out = _out_proj(ya.reshape(bsz * seq, A_WIDTH), yb.reshape(bsz * seq, B_WIDTH),
                    w_out.astype(BF16), x2, mod3, g_post, seq)
    return out.reshape(bsz, seq, d)


def kernel(x, c, w_ada, b_ada, g_pre, w_in, g_q_lora, w_uq, g_kv_lora, w_ukv, w_out, g_post):
    for layer in range(w_ada.shape[0]):
        x = _layer(x, c, w_ada[layer], b_ada[layer], g_pre[layer], w_in[layer], g_q_lora[layer],
                   w_uq[layer], g_kv_lora[layer], w_ukv[layer], w_out[layer], g_post[layer])
    return x
```

```python
import jax
import jax.numpy as jnp
import numpy as np
from jax import lax
from jax.experimental import pallas as pl
from jax.experimental.pallas import tpu as pltpu

F32 = jnp.float32
BF16 = jnp.bfloat16

D_MODEL = 2048
A_HEADS = 8
A_HEAD_DIM = 128
A_WIDTH = A_HEADS * A_HEAD_DIM
DILATED_PATTERNS = ((128, 1), (512, 4), (2048, 16))
B_HEADS = 8
B_V_DIM = 128
B_WIDTH = B_HEADS * B_V_DIM
B_NOPE_DIM = 128
B_ROPE_DIM = 64
Q_LORA_RANK = 512
KV_LORA_RANK = 256
ROPE_THETA = 10000.0
NORM_EPS = 1e-6
NEG_INF = -1e30
LOG2E = 1.4426950408889634

LANES = 128
B_QK_PAD = 256
KPE_PAD = LANES

OFF_AZ = 3 * A_WIDTH
OFF_CQ = OFF_AZ + A_WIDTH
OFF_CKV = OFF_CQ + Q_LORA_RANK

ROW_TILE = 512
BAND_Q = 128
BAND_HEAD_GROUP = 2
MLA_Q = 256
MLA_HEAD_GROUP = 2
VMEM_LIMIT = 56 * 1024 * 1024


def _silu(v):
    return v * (1.0 / (1.0 + jnp.exp(-v)))


def _rms(v, g):
    return v * lax.rsqrt(jnp.mean(v * v, axis=-1, keepdims=True) + NORM_EPS) * g


def _dot_nt(a, b):
    return lax.dot_general(a, b, (((1,), (1,)), ((), ())), preferred_element_type=F32)


def _ada_kernel(c_ref, w_ref, b_ref, o_ref):
    s = _silu(c_ref[...]).astype(BF16)
    o_ref[...] = jnp.dot(s, w_ref[...].astype(BF16), preferred_element_type=F32) + b_ref[...]


def _ada(c, w_ada, b_ada):
    bsz, d = c.shape
    n = w_ada.shape[1]
    tn = 1024
    return pl.pallas_call(
        _ada_kernel,
        grid=(n // tn,),
        in_specs=[
            pl.BlockSpec((bsz, d), lambda j: (0, 0)),
            pl.BlockSpec((d, tn), lambda j: (0, j)),
            pl.BlockSpec((1, tn), lambda j: (0, j)),
        ],
        out_specs=pl.BlockSpec((bsz, tn), lambda j: (0, j)),
        out_shape=jax.ShapeDtypeStruct((bsz, n), F32),
        compiler_params=pltpu.CompilerParams(dimension_semantics=("arbitrary",),
                                             vmem_limit_bytes=VMEM_LIMIT),
        name="ada_mod",
    )(c, w_ada, b_ada.reshape(1, n))


def _cast_w_in_kernel(w_ref, qkv_ref, mid_ref, bz_ref, kpe_ref):
    kpe0 = OFF_CKV + KV_LORA_RANK
    qkv_ref[...] = w_ref[:OFF_AZ, :].astype(BF16)
    mid_ref[...] = w_ref[OFF_AZ:kpe0, :].astype(BF16)
    bz_ref[...] = w_ref[kpe0 + B_ROPE_DIM:, :].astype(BF16)
    kpe_ref[:B_ROPE_DIM, :] = w_ref[kpe0:kpe0 + B_ROPE_DIM, :].astype(BF16)
    kpe_ref[B_ROPE_DIM:, :] = jnp.zeros((KPE_PAD - B_ROPE_DIM, w_ref.shape[1]), BF16)


def _cast_w_in(w_in_t):
    n, d = w_in_t.shape
    tc = 256
    heights = (OFF_AZ, OFF_CKV + KV_LORA_RANK - OFF_AZ, B_WIDTH, KPE_PAD)
    return pl.pallas_call(
        _cast_w_in_kernel,
        grid=(d // tc,),
        in_specs=[pl.BlockSpec((n, tc), lambda j: (0, j))],
        out_specs=[pl.BlockSpec((hgt, tc), lambda j: (0, j)) for hgt in heights],
        out_shape=[jax.ShapeDtypeStruct((hgt, d), BF16) for hgt in heights],
        compiler_params=pltpu.CompilerParams(dimension_semantics=("arbitrary",),
                                             vmem_limit_bytes=VMEM_LIMIT),
        name="cast_w_in",
    )(w_in_t)


def _qkv_kernel(x_ref, mod_ref, g_ref, w_ref, h_ref, s1_ref, s4_ref, s16_ref, acc_ref, acc4_ref):
    tm = x_ref.shape[0]
    d4 = DILATED_PATTERNS[1][1]
    x = x_ref[...]
    gain = g_ref[...] * (1.0 + mod_ref[0, 1:2, :])
    inv = lax.rsqrt(jnp.mean(x * x, axis=-1, keepdims=True) + NORM_EPS)
    h_ref[...] = (x * inv * gain + mod_ref[0, 0:1, :]).astype(BF16)

    heads_per = acc_ref.shape[0]
    chunk = heads_per * A_HEAD_DIM
    q_scale = A_HEAD_DIM ** -0.5 * LOG2E
    for t in range(3):
        for c0 in range(0, A_WIDTH, chunk):
            acc = _dot_nt(h_ref[...], w_ref[t * A_WIDTH + c0:t * A_WIDTH + c0 + chunk, :])
            if t == 0:
                acc = acc * q_scale
            for k in range(heads_per):
                head = c0 // A_HEAD_DIM + k
                a = acc[:, k * A_HEAD_DIM:(k + 1) * A_HEAD_DIM]
                s1_ref[t, 0, head, :, :] = a.astype(BF16)
                acc_ref[k, :, :] = a
            for k in range(heads_per):
                head = c0 // A_HEAD_DIM + k
                for r4 in range(d4):
                    g4 = acc_ref[k, pl.ds(r4, tm // d4, stride=d4), :]
                    s4_ref[t, 0, head, :, r4 * LANES:(r4 + 1) * LANES] = g4.astype(BF16)
                    acc4_ref[k, r4, :, :] = g4
                for r4 in range(d4):
                    for a4 in range(d4):
                        r = r4 + d4 * a4
                        s16_ref[t, 0, head, :, r * LANES:(r + 1) * LANES] = (
                            acc4_ref[k, r4, pl.ds(a4, tm // (d4 * d4), stride=d4), :].astype(BF16))


def _qkv_proj(x2, mod3, g_pre, w_qkv, bsz, seq):
    n_tok, d = x2.shape
    tm = ROW_TILE
    per_b = seq // tm
    hd = A_HEAD_DIM
    d4, d16 = DILATED_PATTERNS[1][1], DILATED_PATTERNS[2][1]
    assert d16 == d4 * d4
    head_major = lambda i: (0, i // per_b, 0, i % per_b, 0)
    return pl.pallas_call(
        _qkv_kernel,
        grid=(n_tok // tm,),
        in_specs=[
            pl.BlockSpec((tm, d), lambda i: (i, 0)),
            pl.BlockSpec((1, 3, d), lambda i: (i // per_b, 0, 0)),
            pl.BlockSpec((1, d), lambda i: (0, 0)),
            pl.BlockSpec(w_qkv.shape, lambda i: (0, 0), pipeline_mode=pl.Buffered(1)),
        ],
        out_specs=[
            pl.BlockSpec((tm, d), lambda i: (i, 0)),
            pl.BlockSpec((3, 1, A_HEADS, tm, hd), head_major),
            pl.BlockSpec((3, 1, A_HEADS, tm // d4, d4 * hd), head_major),
            pl.BlockSpec((3, 1, A_HEADS, tm // d16, d16 * hd), head_major),
        ],
        out_shape=[
            jax.ShapeDtypeStruct((n_tok, d), BF16),
            jax.ShapeDtypeStruct((3, bsz, A_HEADS, seq, hd), BF16),
            jax.ShapeDtypeStruct((3, bsz, A_HEADS, seq // d4, d4 * hd), BF16),
            jax.ShapeDtypeStruct((3, bsz, A_HEADS, seq // d16, d16 * hd), BF16),
        ],
        scratch_shapes=[pltpu.VMEM((4, tm, hd), F32), pltpu.VMEM((4, d4, tm // d4, hd), F32)],
        compiler_params=pltpu.CompilerParams(dimension_semantics=("arbitrary",),
                                             vmem_limit_bytes=VMEM_LIMIT),
        name="qkv_proj",
    )(x2, mod3, g_pre.reshape(1, d), w_qkv)


def _band_tables():
    bq, win = BAND_Q, 2 * BAND_Q
    radius = DILATED_PATTERNS[0][0] // 2 // DILATED_PATTERNS[0][1]
    rows = np.arange(bq)[:, None]
    cols = np.arange(win)[None, :]
    tabs = []
    for k_minus_q in (0, -radius, -bq, 0):
        dist = np.abs(cols + k_minus_q - rows)
        tabs.append(np.where(dist <= radius, -LOG2E * dist, NEG_INF))
    tabs[3][:, bq:] = NEG_INF
    return np.stack(tabs).astype(np.float32)


def _band_block(q, k_win, v_win, bias):
    s = _dot_nt(q, k_win) + bias
    m = jnp.max(s, axis=-1, keepdims=True)
    p = jnp.exp2(s - m).astype(BF16)
    v_ext = jnp.concatenate([v_win, jnp.ones_like(v_win)], axis=1)
    ol = jnp.dot(p, v_ext, preferred_element_type=F32)
    return ol[:, :A_HEAD_DIM], m, ol[:, A_HEAD_DIM:]


def _dil_kernel(slope_ref, tab_ref, s1_ref, s4_ref, s16_ref, zg_ref, y_ref,
                acc_ref, m_ref, l_ref, stage_ref):
    heads = s1_ref.shape[2]
    seq = s1_ref.shape[3]
    bq = BAND_Q
    win = 2 * bq
    views = (s1_ref, s4_ref, s16_ref)
    stats = (acc_ref, m_ref, l_ref)
    d4 = DILATED_PATTERNS[1][1]

    def blocks_of(hh, p_idx):
        window, dil = DILATED_PATTERNS[p_idx]
        radius = window // 2 // dil
        length = seq // dil
        ref = views[p_idx]
        pen = slope_ref[pl.program_id(1) * heads + hh] * float(dil)
        n_blk = length // bq
        bias = ([pen * tab_ref[3, :, :bq]] if n_blk == 1 else
                [pen * tab_ref[kind] for kind in range(3)])
        for r in sorted(range(dil), key=lambda r: (r % d4, r)):
            cols = slice(r * LANES, (r + 1) * LANES)
            if n_blk == 1:
                yield 0, r, _band_block(ref[0, 0, hh, :, cols], ref[1, 0, hh, :, cols],
                                        ref[2, 0, hh, :, cols], bias[0])
                continue
            for qi in range(n_blk):
                kind = 0 if qi == 0 else (2 if qi == n_blk - 1 else 1)
                q_lo = qi * bq
                k_lo = min(max(q_lo - radius, 0), length - win)
                yield q_lo, r, _band_block(ref[0, 0, hh, pl.ds(q_lo, bq), cols],
                                           ref[1, 0, hh, pl.ds(k_lo, win), cols],
                                           ref[2, 0, hh, pl.ds(k_lo, win), cols], bias[kind])

    def strided_phase(hh):
        for p_idx in range(1, len(DILATED_PATTERNS)):
            dil = DILATED_PATTERNS[p_idx][1]
            for q_lo, r, (acc, m, l) in blocks_of(hh, p_idx):
                vals = (acc, jnp.broadcast_to(m, acc.shape), l)
                if dil == d4:
                    rows = pl.ds(q_lo * dil + r, bq, stride=dil)
                    for ref, val in zip(stats, vals):
                        ref[hh, p_idx - 1, rows, :] = val
                else:
                    r4, a4 = r % d4, r // d4
                    for k, val in enumerate(vals):
                        stage_ref[hh, k, r4, pl.ds(a4, bq, stride=d4), :] = val
                    if a4 == d4 - 1:
                        for k, ref in enumerate(stats):
                            ref[hh, p_idx - 1, pl.ds(r4, seq // d4, stride=d4), :] = (
                                stage_ref[hh, k, r4, :, :])
                yield

    def mix_phase(hh):
        out_cols = slice(hh * A_HEAD_DIM, (hh + 1) * A_HEAD_DIM)
        for q_lo, _, (acc0, m0, l0) in blocks_of(hh, 0):
            rs = pl.ds(q_lo, bq)
            m1, m2 = m_ref[hh, 0, rs, :], m_ref[hh, 1, rs, :]
            m = jnp.maximum(jnp.maximum(m0, m1), m2)
            e0, e1, e2 = jnp.exp2(m0 - m), jnp.exp2(m1 - m), jnp.exp2(m2 - m)
            num = e0 * acc0 + e1 * acc_ref[hh, 0, rs, :] + e2 * acc_ref[hh, 1, rs, :]
            den = e0 * l0 + e1 * l_ref[hh, 0, rs, :] + e2 * l_ref[hh, 1, rs, :]
            y_ref[0, rs, out_cols] = (num / den * zg_ref[0, rs, out_cols].astype(F32)).astype(BF16)
            yield

    prev_mix = None
    for hh in range(heads):
        cur = strided_phase(hh)
        if prev_mix is None:
            for _ in cur:
                pass
        else:
            for _ in cur:
                next(prev_mix, None)
                next(cur, None)
            for _ in prev_mix:
                pass
        prev_mix = mix_phase(hh)
    for _ in prev_mix:
        pass


def _dilated(views, zg3, slopes):
    _, bsz, heads, seq, hd = views[0].shape
    assert DILATED_PATTERNS[0][1] == 1 and len(views) == len(DILATED_PATTERNS)
    grp = BAND_HEAD_GROUP
    d4 = DILATED_PATTERNS[1][1]
    assert [p[1] for p in DILATED_PATTERNS] == [1, d4, d4 * d4]
    tables = _band_tables()
    specs = [pl.BlockSpec((3, 1, grp) + v.shape[3:], lambda b, g: (0, b, g, 0, 0)) for v in views]
    tok_spec = pl.BlockSpec((1, seq, grp * hd), lambda b, g: (b, 0, g))
    n_strided = len(DILATED_PATTERNS) - 1
    return pl.pallas_call(
        _dil_kernel,
        grid=(bsz, heads // grp),
        in_specs=[pl.BlockSpec(memory_space=pltpu.SMEM),
                  pl.BlockSpec(tables.shape, lambda b, g: (0, 0, 0))] + specs + [tok_spec],
        out_specs=tok_spec,
        out_shape=jax.ShapeDtypeStruct((bsz, seq, heads * hd), BF16),
        scratch_shapes=([pltpu.VMEM((grp, n_strided, seq, hd), F32) for _ in range(3)]
                        + [pltpu.VMEM((grp, 3, d4, seq // d4, hd), F32)]),
        compiler_params=pltpu.CompilerParams(dimension_semantics=("arbitrary", "arbitrary"),
                                             vmem_limit_bytes=VMEM_LIMIT),
        name="dilated_mix",
    )(slopes, jnp.asarray(tables), *views, zg3)


def _rope(pe, cc, sa, sb):
    return pe * cc + pltpu.roll(pe, 96, 1) * sa + pltpu.roll(pe, 32, 1) * sb


def _mla_proj_kernel(h_ref, wmid_ref, wbz_ref, wkpe_ref, gq_ref, gkv_ref, wqn_ref, wqp_ref, wkv_ref,
                     cc_ref, sa_ref, sb_ref, zg_ref, bzg_ref, qf_ref, kf_ref, v_ref):
    def proj(w_ref, off, width):
        return _dot_nt(h_ref[...], w_ref[off:off + width, :])

    chunk = 4 * LANES
    for c0 in range(0, A_WIDTH, chunk):
        zg_ref[:, c0:c0 + chunk] = _silu(proj(wmid_ref, c0, chunk)).astype(BF16)
    for c0 in range(0, B_WIDTH, chunk):
        bzg_ref[:, c0:c0 + chunk] = _silu(proj(wbz_ref, c0, chunk)).astype(BF16)

    cc, sa, sb = cc_ref[...], sa_ref[...], sb_ref[...]
    q_scale = (B_NOPE_DIM + B_ROPE_DIM) ** -0.5 * LOG2E
    cqn = _rms(proj(wmid_ref, OFF_CQ - OFF_AZ, Q_LORA_RANK), gq_ref[...]).astype(BF16)
    low_half = lax.broadcasted_iota(jnp.int32, (1, LANES), 1) < B_ROPE_DIM
    q_pe = jnp.dot(cqn, wqp_ref[...], preferred_element_type=F32)
    for h0 in range(0, B_HEADS, 2):
        q_nope = jnp.dot(cqn, wqn_ref[:, h0 * B_NOPE_DIM:(h0 + 2) * B_NOPE_DIM],
                         preferred_element_type=F32) * q_scale
        q_rot = _rope(q_pe[:, h0 * B_ROPE_DIM:(h0 + 2) * B_ROPE_DIM], cc, sa, sb) * q_scale
        for h in (h0, h0 + 1):
            own = low_half if h == h0 else jnp.logical_not(low_half)
            qf_ref[0, h, :, :B_NOPE_DIM] = (
                q_nope[:, (h - h0) * B_NOPE_DIM:(h - h0 + 1) * B_NOPE_DIM].astype(BF16))
            qf_ref[0, h, :, B_NOPE_DIM:] = jnp.where(own, q_rot, 0.0).astype(BF16)
    k_rot_low = _rope(proj(wkpe_ref, 0, KPE_PAD), cc, sa, sb)
    k_rot = (k_rot_low.astype(BF16), pltpu.roll(k_rot_low, B_ROPE_DIM, 1).astype(BF16))
    ckvn = _rms(proj(wmid_ref, OFF_CKV - OFF_AZ, KV_LORA_RANK), gkv_ref[...]).astype(BF16)
    kv_w = B_NOPE_DIM + B_V_DIM
    for h in range(B_HEADS):
        kv = jnp.dot(ckvn, wkv_ref[:, h * kv_w:(h + 1) * kv_w], preferred_element_type=F32)
        kf_ref[0, h, :, :B_NOPE_DIM] = kv[:, :B_NOPE_DIM].astype(BF16)
        kf_ref[0, h, :, B_NOPE_DIM:] = k_rot[h % 2]
        v_ref[0, h, :, :] = kv[:, B_NOPE_DIM:].astype(BF16)


def _mla_proj(h, w_mid, w_bz, w_kpe, g_q, g_kv, wq_nope, wq_pe, wkv, cc, sa, sb, bsz, seq):
    n_tok, d = h.shape
    tm = ROW_TILE
    per_b = seq // tm
    row = lambda i: (i, 0)
    full = lambda i: (0, 0)
    pos = lambda i: (i % per_b, 0)
    head_major = lambda i: (i // per_b, 0, i % per_b, 0)
    resident = lambda w: pl.BlockSpec(w.shape, full, pipeline_mode=pl.Buffered(1))
    return pl.pallas_call(
        _mla_proj_kernel,
        grid=(n_tok // tm,),
        in_specs=[
            pl.BlockSpec((tm, d), row),
            resident(w_mid), resident(w_bz), resident(w_kpe),
            pl.BlockSpec((1, Q_LORA_RANK), full),
            pl.BlockSpec((1, KV_LORA_RANK), full),
            resident(wq_nope), resident(wq_pe), resident(wkv),
            pl.BlockSpec((tm, LANES), pos),
            pl.BlockSpec((tm, LANES), pos),
            pl.BlockSpec((tm, LANES), pos),
        ],
        out_specs=[
            pl.BlockSpec((tm, A_WIDTH), row),
            pl.BlockSpec((tm, B_WIDTH), row),
            pl.BlockSpec((1, B_HEADS, tm, B_QK_PAD), head_major),
            pl.BlockSpec((1, B_HEADS, tm, B_QK_PAD), head_major),
            pl.BlockSpec((1, B_HEADS, tm, B_V_DIM), head_major),
        ],
        out_shape=[
            jax.ShapeDtypeStruct((n_tok, A_WIDTH), BF16),
            jax.ShapeDtypeStruct((n_tok, B_WIDTH), BF16),
            jax.ShapeDtypeStruct((bsz, B_HEADS, seq, B_QK_PAD), BF16),
            jax.ShapeDtypeStruct((bsz, B_HEADS, seq, B_QK_PAD), BF16),
            jax.ShapeDtypeStruct((bsz, B_HEADS, seq, B_V_DIM), BF16),
        ],
        compiler_params=pltpu.CompilerParams(dimension_semantics=("arbitrary",),
                                             vmem_limit_bytes=VMEM_LIMIT),
        name="mla_proj",
    )(h, w_mid, w_bz, w_kpe, g_q.reshape(1, -1), g_kv.reshape(1, -1), wq_nope, wq_pe, wkv,
      cc, sa, sb)


def _mla_attn_kernel(q_ref, k_ref, v_ref, zg_ref, y_ref, vext_ref, s_ref):
    heads = q_ref.shape[1]
    seq = k_ref.shape[2]
    n_blk = seq // MLA_Q
    for hh in range(heads):
        vext_ref[hh, :, :B_V_DIM] = v_ref[0, hh, :, :]
        vext_ref[hh, :, B_V_DIM:] = jnp.ones((seq, B_V_DIM), BF16)

    def scores(hh, blk, slot):
        s_ref[slot] = _dot_nt(q_ref[0, hh, pl.ds(blk * MLA_Q, MLA_Q), :], k_ref[0, hh, :, :])

    def softmax_pv(hh, blk, slot):
        s = s_ref[slot]
        m = jnp.max(s, axis=-1, keepdims=True)
        p = jnp.exp2(s - m).astype(BF16)
        ol = jnp.dot(p, vext_ref[hh], preferred_element_type=F32)
        o = ol[:, :B_V_DIM] / ol[:, B_V_DIM:]
        rows = pl.ds(blk * MLA_Q, MLA_Q)
        cols = slice(hh * B_V_DIM, (hh + 1) * B_V_DIM)
        y_ref[0, rows, cols] = (o * zg_ref[0, rows, cols].astype(F32)).astype(BF16)

    items = [(hh, blk) for hh in range(heads) for blk in range(n_blk)]
    scores(*items[0], 0)
    for n, item in enumerate(items):
        if n + 1 < len(items):
            scores(*items[n + 1], (n + 1) % 2)
        softmax_pv(*item, n % 2)


def _mla_attn(qf, kf, v, bzg3):
    bsz, heads, seq, _ = qf.shape
    grp = MLA_HEAD_GROUP
    head_blk = lambda b, g: (b, g, 0, 0)
    tok_spec = pl.BlockSpec((1, seq, grp * B_V_DIM), lambda b, g: (b, 0, g))
    return pl.pallas_call(
        _mla_attn_kernel,
        grid=(bsz, heads // grp),
        in_specs=[
            pl.BlockSpec((1, grp, seq, B_QK_PAD), head_blk),
            pl.BlockSpec((1, grp, seq, B_QK_PAD), head_blk),
            pl.BlockSpec((1, grp, seq, B_V_DIM), head_blk),
            tok_spec,
        ],
        out_specs=tok_spec,
        out_shape=jax.ShapeDtypeStruct((bsz, seq, heads * B_V_DIM), BF16),
        scratch_shapes=[pltpu.VMEM((grp, seq, 2 * B_V_DIM), BF16), pltpu.VMEM((2, MLA_Q, seq), F32)],
        compiler_params=pltpu.CompilerParams(dimension_semantics=("arbitrary", "arbitrary"),
                                             vmem_limit_bytes=VMEM_LIMIT),
        name="mla_attn",
    )(qf, kf, v, bzg3)


def _out_kernel(ya_ref, yb_ref, w_ref, x_ref, mod_ref, g_ref, o_ref, y_ref):
    tm, d = y_ref.shape
    chunk = 4 * LANES
    ssq = jnp.zeros((tm, LANES), F32)
    for c0 in range(0, d, chunk):
        y = (jnp.dot(ya_ref[...], w_ref[:A_WIDTH, c0:c0 + chunk], preferred_element_type=F32)
             + jnp.dot(yb_ref[...], w_ref[A_WIDTH:, c0:c0 + chunk], preferred_element_type=F32))
        y_ref[:, c0:c0 + chunk] = y
        for t0 in range(0, chunk, LANES):
            ssq = ssq + y[:, t0:t0 + LANES] * y[:, t0:t0 + LANES]
    inv = lax.rsqrt(jnp.sum(ssq, axis=-1, keepdims=True) * (1.0 / d) + NORM_EPS)
    gain = mod_ref[0, 2:3, :] * g_ref[...]
    o_ref[...] = x_ref[...] + y_ref[...] * inv * gain


def _out_proj(ya, yb, w_out, x2, mod3, g_post, seq):
    n_tok, d = x2.shape
    tm = ROW_TILE
    per_b = seq // tm
    row = lambda i: (i, 0)
    return pl.pallas_call(
        _out_kernel,
        grid=(n_tok // tm,),
        in_specs=[
            pl.BlockSpec((tm, A_WIDTH), row),
            pl.BlockSpec((tm, B_WIDTH), row),
            pl.BlockSpec(w_out.shape, lambda i: (0, 0), pipeline_mode=pl.Buffered(1)),
            pl.BlockSpec((tm, d), row),
            pl.BlockSpec((1, 3, d), lambda i: (i // per_b, 0, 0)),
            pl.BlockSpec((1, d), lambda i: (0, 0)),
        ],
        out_specs=pl.BlockSpec((tm, d), row),
        out_shape=jax.ShapeDtypeStruct((n_tok, d), F32),
        scratch_shapes=[pltpu.VMEM((tm, d), F32)],
        compiler_params=pltpu.CompilerParams(dimension_semantics=("arbitrary",),
                                             vmem_limit_bytes=VMEM_LIMIT),
        name="out_proj",
    )(ya, yb, w_out, x2, mod3, g_post.reshape(1, d))


def _split_w_uq(w_uq):
    r = w_uq.shape[0]
    w = w_uq.reshape(r, B_HEADS, B_NOPE_DIM + B_ROPE_DIM).astype(BF16)
    return (w[:, :, :B_NOPE_DIM].reshape(r, B_HEADS * B_NOPE_DIM),
            w[:, :, B_NOPE_DIM:].reshape(r, B_HEADS * B_ROPE_DIM))


def _rope_tables(seq):
    half = B_ROPE_DIM // 2
    pos = jnp.arange(seq, dtype=F32)
    inv_freq = jnp.power(ROPE_THETA, -jnp.arange(half, dtype=F32) / half)
    ang = pos[:, None] * inv_freq[None, :]
    cos, sin = jnp.cos(ang), jnp.sin(ang)
    z = jnp.zeros_like(cos)
    cc = jnp.concatenate([cos, cos, cos, cos], axis=1)
    sa = jnp.concatenate([-sin, z, -sin, z], axis=1)
    sb = jnp.concatenate([z, sin, z, sin], axis=1)
    return cc, sa, sb


def _layer(x, c, w_ada, b_ada, g_pre, w_in, g_q_lora, w_uq, g_kv_lora, w_ukv, w_out, g_post):
    bsz, seq, d = x.shape
    x2 = x.reshape(bsz * seq, d)
    mod3 = _ada(c, w_ada, b_ada).reshape(bsz, 3, d)
    w_qkv, w_mid, w_bz, w_kpe = _cast_w_in(w_in.T)
    h, s1, s4, s16 = _qkv_proj(x2, mod3, g_pre, w_qkv, bsz, seq)
    cc, sa, sb = _rope_tables(seq)
    zg, bzg, qf, kf, v = _mla_proj(h, w_mid, w_bz, w_kpe, g_q_lora, g_kv_lora, *_split_w_uq(w_uq),
                                   w_ukv.astype(BF16), cc, sa, sb, bsz, seq)
    slopes = jnp.exp2(-8.0 * jnp.arange(1, A_HEADS + 1, dtype=F32) / A_HEADS)
    ya = _dilated((s1, s4, s16), zg.reshape(bsz, seq, A_WIDTH), slopes)
    yb = _mla_attn(qf, kf, v, bzg.reshape(bsz, seq, B_WIDTH))
    out = _out_proj(ya.reshape(bsz * seq, A_WIDTH), yb.reshape(bsz * seq, B_WIDTH),
                    w_out.astype(BF16), x2, mod3, g_post, seq)
    return out.reshape(bsz, seq, d)


def kernel(x, c, w_ada, b_ada, g_pre, w_in, g_q_lora, w_uq, g_kv_lora, w_ukv, w_out, g_post):
    for layer in range(w_ada.shape[0]):
        x = _layer(x, c, w_ada[layer], b_ada[layer], g_pre[layer], w_in[layer], g_q_lora[layer],
                   w_uq[layer], g_kv_lora[layer], w_ukv[layer], w_out[layer], g_post[layer])
    return x
```

```python
import jax
import jax.numpy as jnp
import numpy as np
from jax import lax
from jax.experimental import pallas as pl
from jax.experimental.pallas import tpu as pltpu

F32 = jnp.float32
BF16 = jnp.bfloat16

D_MODEL = 2048
A_HEADS = 8
A_HEAD_DIM = 128
A_WIDTH = A_HEADS * A_HEAD_DIM
DILATED_PATTERNS = ((128, 1), (512, 4), (2048, 16))
B_HEADS = 8
B_V_DIM = 128
B_WIDTH = B_HEADS * B_V_DIM
B_NOPE_DIM = 128
B_ROPE_DIM = 64
Q_LORA_RANK = 512
KV_LORA_RANK = 256
ROPE_THETA = 10000.0
NORM_EPS = 1e-6
NEG_INF = -1e30
LOG2E = 1.4426950408889634

LANES = 128
B_QK_PAD = 256
KPE_PAD = LANES

OFF_AZ = 3 * A_WIDTH
OFF_CQ = OFF_AZ + A_WIDTH
OFF_CKV = OFF_CQ + Q_LORA_RANK

ADA_COL_TILE = 1024
CAST_COL_TILE = 512
ROW_TILE = 512
QKV_HEADS_PER_DOT = 8
PROJ_DOT_COLS = 1024
BAND_Q = 128
BAND_HEAD_GROUP = 2
MLA_Q = 256
MLA_HEAD_GROUP = 2
VMEM_LIMIT = 56 * 1024 * 1024


def _params(grid_rank):
    return pltpu.CompilerParams(dimension_semantics=("arbitrary",) * grid_rank,
                                vmem_limit_bytes=VMEM_LIMIT)


def _silu(v):
    return v * (1.0 / (1.0 + jnp.exp(-v)))


def _rms(v, g):
    return v * lax.rsqrt(jnp.mean(v * v, axis=-1, keepdims=True) + NORM_EPS) * g


def _dot_nt(a, b):
    return lax.dot_general(a, b, (((1,), (1,)), ((), ())), preferred_element_type=F32)


def _ada_kernel(c_ref, w_ref, b_ref, o_ref):
    s = _silu(c_ref[...]).astype(BF16)
    o_ref[...] = jnp.dot(s, w_ref[...].astype(BF16), preferred_element_type=F32) + b_ref[...]


def _ada(c, w_ada, b_ada):
    bsz, d = c.shape
    n = w_ada.shape[1]
    tn = ADA_COL_TILE
    return pl.pallas_call(
        _ada_kernel,
        grid=(n // tn,),
        in_specs=[
            pl.BlockSpec((bsz, d), lambda j: (0, 0)),
            pl.BlockSpec((d, tn), lambda j: (0, j)),
            pl.BlockSpec((1, tn), lambda j: (0, j)),
        ],
        out_specs=pl.BlockSpec((bsz, tn), lambda j: (0, j)),
        out_shape=jax.ShapeDtypeStruct((bsz, n), F32),
        compiler_params=_params(1),
        name="ada_mod",
    )(c, w_ada, b_ada.reshape(1, n))


def _cast_w_in_kernel(w_ref, qkv_ref, mid_ref, bz_ref, kpe_ref):
    kpe0 = OFF_CKV + KV_LORA_RANK
    qkv_ref[...] = w_ref[:OFF_AZ, :].astype(BF16)
    mid_ref[...] = w_ref[OFF_AZ:kpe0, :].astype(BF16)
    bz_ref[...] = w_ref[kpe0 + B_ROPE_DIM:, :].astype(BF16)
    kpe_ref[:B_ROPE_DIM, :] = w_ref[kpe0:kpe0 + B_ROPE_DIM, :].astype(BF16)
    kpe_ref[B_ROPE_DIM:, :] = jnp.zeros((KPE_PAD - B_ROPE_DIM, w_ref.shape[1]), BF16)


def _cast_w_in(w_in_t):
    n, d = w_in_t.shape
    tc = CAST_COL_TILE
    heights = (OFF_AZ, OFF_CKV + KV_LORA_RANK - OFF_AZ, B_WIDTH, KPE_PAD)
    return pl.pallas_call(
        _cast_w_in_kernel,
        grid=(d // tc,),
        in_specs=[pl.BlockSpec((n, tc), lambda j: (0, j))],
        out_specs=[pl.BlockSpec((hgt, tc), lambda j: (0, j)) for hgt in heights],
        out_shape=[jax.ShapeDtypeStruct((hgt, d), BF16) for hgt in heights],
        compiler_params=_params(1),
        name="cast_w_in",
    )(w_in_t)


def _qkv_kernel(x_ref, mod_ref, g_ref, w_ref, h_ref, s1_ref, s4_ref, s16_ref, acc_ref, acc4_ref):
    tm = x_ref.shape[0]
    d4 = DILATED_PATTERNS[1][1]
    x = x_ref[...]
    gain = g_ref[...] * (1.0 + mod_ref[0, 1:2, :])
    inv = lax.rsqrt(jnp.mean(x * x, axis=-1, keepdims=True) + NORM_EPS)
    h_ref[...] = (x * inv * gain + mod_ref[0, 0:1, :]).astype(BF16)

    heads_per = acc_ref.shape[0]
    chunk = heads_per * A_HEAD_DIM
    q_scale = A_HEAD_DIM ** -0.5 * LOG2E
    for t in range(3):
        for c0 in range(0, A_WIDTH, chunk):
            acc = _dot_nt(h_ref[...], w_ref[t * A_WIDTH + c0:t * A_WIDTH + c0 + chunk, :])
            if t == 0:
                acc = acc * q_scale
            for k in range(heads_per):
                head = c0 // A_HEAD_DIM + k
                a = acc[:, k * A_HEAD_DIM:(k + 1) * A_HEAD_DIM]
                s1_ref[t, 0, head, :, :] = a.astype(BF16)
                acc_ref[k, :, :] = a
            for k in range(heads_per):
                head = c0 // A_HEAD_DIM + k
                for r4 in range(d4):
                    g4 = acc_ref[k, pl.ds(r4, tm // d4, stride=d4), :]
                    s4_ref[t, 0, head, :, r4 * LANES:(r4 + 1) * LANES] = g4.astype(BF16)
                    acc4_ref[k, r4, :, :] = g4
                for r4 in range(d4):
                    for a4 in range(d4):
                        r = r4 + d4 * a4
                        s16_ref[t, 0, head, :, r * LANES:(r + 1) * LANES] = (
                            acc4_ref[k, r4, pl.ds(a4, tm // (d4 * d4), stride=d4), :].astype(BF16))


def _qkv_proj(x2, mod3, g_pre, w_qkv, bsz, seq):
    n_tok, d = x2.shape
    tm = ROW_TILE
    per_b = seq // tm
    hd = A_HEAD_DIM
    d4, d16 = DILATED_PATTERNS[1][1], DILATED_PATTERNS[2][1]
    assert d16 == d4 * d4
    head_major = lambda i: (0, i // per_b, 0, i % per_b, 0)
    return pl.pallas_call(
        _qkv_kernel,
        grid=(n_tok // tm,),
        in_specs=[
            pl.BlockSpec((tm, d), lambda i: (i, 0)),
            pl.BlockSpec((1, 3, d), lambda i: (i // per_b, 0, 0)),
            pl.BlockSpec((1, d), lambda i: (0, 0)),
            pl.BlockSpec(w_qkv.shape, lambda i: (0, 0), pipeline_mode=pl.Buffered(1)),
        ],
        out_specs=[
            pl.BlockSpec((tm, d), lambda i: (i, 0)),
            pl.BlockSpec((3, 1, A_HEADS, tm, hd), head_major),
            pl.BlockSpec((3, 1, A_HEADS, tm // d4, d4 * hd), head_major),
            pl.BlockSpec((3, 1, A_HEADS, tm // d16, d16 * hd), head_major),
        ],
        out_shape=[
            jax.ShapeDtypeStruct((n_tok, d), BF16),
            jax.ShapeDtypeStruct((3, bsz, A_HEADS, seq, hd), BF16),
            jax.ShapeDtypeStruct((3, bsz, A_HEADS, seq // d4, d4 * hd), BF16),
            jax.ShapeDtypeStruct((3, bsz, A_HEADS, seq // d16, d16 * hd), BF16),
        ],
        scratch_shapes=[pltpu.VMEM((QKV_HEADS_PER_DOT, tm, hd), F32),
                        pltpu.VMEM((QKV_HEADS_PER_DOT, d4, tm // d4, hd), F32)],
        compiler_params=_params(1),
        name="qkv_proj",
    )(x2, mod3, g_pre.reshape(1, d), w_qkv)


def _band_tables():
    bq, win = BAND_Q, 2 * BAND_Q
    radius = DILATED_PATTERNS[0][0] // 2 // DILATED_PATTERNS[0][1]
    rows = np.arange(bq)[:, None]
    cols = np.arange(win)[None, :]
    tabs = []
    for k_minus_q in (0, -radius, -bq, 0):
        dist = np.abs(cols + k_minus_q - rows)
        tabs.append(np.where(dist <= radius, -LOG2E * dist, NEG_INF))
    tabs[3][:, bq:] = NEG_INF
    return np.stack(tabs).astype(np.float32)


def _band_block(q, k_win, v_win, bias):
    s = _dot_nt(q, k_win) + bias
    m = jnp.max(s, axis=-1, keepdims=True)
    p = jnp.exp2(s - m).astype(BF16)
    v_ext = jnp.concatenate([v_win, jnp.ones_like(v_win)], axis=1)
    ol = jnp.dot(p, v_ext, preferred_element_type=F32)
    return ol[:, :A_HEAD_DIM], m, ol[:, A_HEAD_DIM:]


def _dil_kernel(slope_ref, tab_ref, s1_ref, s4_ref, s16_ref, zg_ref, y_ref,
                acc_ref, m_ref, l_ref, stage_ref):
    heads = s1_ref.shape[2]
    seq = s1_ref.shape[3]
    bq = BAND_Q
    win = 2 * bq
    views = (s1_ref, s4_ref, s16_ref)
    stats = (acc_ref, m_ref, l_ref)
    d4 = DILATED_PATTERNS[1][1]

    def blocks_of(hh, p_idx):
        window, dil = DILATED_PATTERNS[p_idx]
        radius = window // 2 // dil
        length = seq // dil
        ref = views[p_idx]
        pen = slope_ref[pl.program_id(1) * heads + hh] * float(dil)
        n_blk = length // bq
        bias = ([pen * tab_ref[3, :, :bq]] if n_blk == 1 else
                [pen * tab_ref[kind] for kind in range(3)])
        for r in sorted(range(dil), key=lambda r: (r % d4, r)):
            cols = slice(r * LANES, (r + 1) * LANES)
            if n_blk == 1:
                yield 0, r, _band_block(ref[0, 0, hh, :, cols], ref[1, 0, hh, :, cols],
                                        ref[2, 0, hh, :, cols], bias[0])
                continue
            for qi in range(n_blk):
                kind = 0 if qi == 0 else (2 if qi == n_blk - 1 else 1)
                q_lo = qi * bq
                k_lo = min(max(q_lo - radius, 0), length - win)
                yield q_lo, r, _band_block(ref[0, 0, hh, pl.ds(q_lo, bq), cols],
                                           ref[1, 0, hh, pl.ds(k_lo, win), cols],
                                           ref[2, 0, hh, pl.ds(k_lo, win), cols], bias[kind])

    def strided_phase(hh):
        for p_idx in range(1, len(DILATED_PATTERNS)):
            dil = DILATED_PATTERNS[p_idx][1]
            for q_lo, r, (acc, m, l) in blocks_of(hh, p_idx):
                vals = (acc, jnp.broadcast_to(m, acc.shape), l)
                if dil == d4:
                    rows = pl.ds(q_lo * dil + r, bq, stride=dil)
                    for ref, val in zip(stats, vals):
                        ref[hh, p_idx - 1, rows, :] = val
                else:
                    r4, a4 = r % d4, r // d4
                    for k, val in enumerate(vals):
                        stage_ref[hh, k, r4, pl.ds(a4, bq, stride=d4), :] = val
                    if a4 == d4 - 1:
                        for k, ref in enumerate(stats):
                            ref[hh, p_idx - 1, pl.ds(r4, seq // d4, stride=d4), :] = (
                                stage_ref[hh, k, r4, :, :])
                yield

    def mix_phase(hh):
        out_cols = slice(hh * A_HEAD_DIM, (hh + 1) * A_HEAD_DIM)
        for q_lo, _, (acc0, m0, l0) in blocks_of(hh, 0):
            rs = pl.ds(q_lo, bq)
            m1, m2 = m_ref[hh, 0, rs, :], m_ref[hh, 1, rs, :]
            m = jnp.maximum(jnp.maximum(m0, m1), m2)
            e0, e1, e2 = jnp.exp2(m0 - m), jnp.exp2(m1 - m), jnp.exp2(m2 - m)
            num = e0 * acc0 + e1 * acc_ref[hh, 0, rs, :] + e2 * acc_ref[hh, 1, rs, :]
            den = e0 * l0 + e1 * l_ref[hh, 0, rs, :] + e2 * l_ref[hh, 1, rs, :]
            y_ref[0, rs, out_cols] = (num / den * zg_ref[0, rs, out_cols].astype(F32)).astype(BF16)
            yield

    prev_mix = None
    for hh in range(heads):
        cur = strided_phase(hh)
        if prev_mix is None:
            for _ in cur:
                pass
        else:
            for _ in cur:
                next(prev_mix, None)
                next(cur, None)
            for _ in prev_mix:
                pass
        prev_mix = mix_phase(hh)
    for _ in prev_mix:
        pass


def _dilated(views, zg3, slopes):
    _, bsz, heads, seq, hd = views[0].shape
    assert DILATED_PATTERNS[0][1] == 1 and len(views) == len(DILATED_PATTERNS)
    grp = BAND_HEAD_GROUP
    d4 = DILATED_PATTERNS[1][1]
    assert [p[1] for p in DILATED_PATTERNS] == [1, d4, d4 * d4]
    tables = _band_tables()
    specs = [pl.BlockSpec((3, 1, grp) + v.shape[3:], lambda b, g: (0, b, g, 0, 0)) for v in views]
    tok_spec = pl.BlockSpec((1, seq, grp * hd), lambda b, g: (b, 0, g))
    n_strided = len(DILATED_PATTERNS) - 1
    return pl.pallas_call(
        _dil_kernel,
        grid=(bsz, heads // grp),
        in_specs=[pl.BlockSpec(memory_space=pltpu.SMEM),
                  pl.BlockSpec(tables.shape, lambda b, g: (0, 0, 0))] + specs + [tok_spec],
        out_specs=tok_spec,
        out_shape=jax.ShapeDtypeStruct((bsz, seq, heads * hd), BF16),
        scratch_shapes=([pltpu.VMEM((grp, n_strided, seq, hd), F32) for _ in range(3)]
                        + [pltpu.VMEM((grp, 3, d4, seq // d4, hd), F32)]),
        compiler_params=_params(2),
        name="dilated_mix",
    )(slopes, jnp.asarray(tables), *views, zg3)


def _rope(pe, cc, sa, sb):
    return pe * cc + pltpu.roll(pe, 96, 1) * sa + pltpu.roll(pe, 32, 1) * sb


def _mla_proj_kernel(h_ref, wmid_ref, wbz_ref, wkpe_ref, gq_ref, gkv_ref, wqn_ref, wqp_ref, wkv_ref,
                     cc_ref, sa_ref, sb_ref, zg_ref, bzg_ref, qf_ref, kf_ref, v_ref):
    def proj(w_ref, off, width):
        return _dot_nt(h_ref[...], w_ref[off:off + width, :])

    chunk = PROJ_DOT_COLS
    for c0 in range(0, A_WIDTH, chunk):
        zg_ref[:, c0:c0 + chunk] = _silu(proj(wmid_ref, c0, chunk)).astype(BF16)
    for c0 in range(0, B_WIDTH, chunk):
        bzg_ref[:, c0:c0 + chunk] = _silu(proj(wbz_ref, c0, chunk)).astype(BF16)

    cc, sa, sb = cc_ref[...], sa_ref[...], sb_ref[...]
    q_scale = (B_NOPE_DIM + B_ROPE_DIM) ** -0.5 * LOG2E
    cqn = _rms(proj(wmid_ref, OFF_CQ - OFF_AZ, Q_LORA_RANK), gq_ref[...]).astype(BF16)
    low_half = lax.broadcasted_iota(jnp.int32, (1, LANES), 1) < B_ROPE_DIM
    q_pe = jnp.dot(cqn, wqp_ref[...], preferred_element_type=F32)
    for h0 in range(0, B_HEADS, 2):
        q_nope = jnp.dot(cqn, wqn_ref[:, h0 * B_NOPE_DIM:(h0 + 2) * B_NOPE_DIM],
                         preferred_element_type=F32) * q_scale
        q_rot = _rope(q_pe[:, h0 * B_ROPE_DIM:(h0 + 2) * B_ROPE_DIM], cc, sa, sb) * q_scale
        for h in (h0, h0 + 1):
            own = low_half if h == h0 else jnp.logical_not(low_half)
            qf_ref[0, h, :, :B_NOPE_DIM] = (
                q_nope[:, (h - h0) * B_NOPE_DIM:(h - h0 + 1) * B_NOPE_DIM].astype(BF16))
            qf_ref[0, h, :, B_NOPE_DIM:] = jnp.where(own, q_rot, 0.0).astype(BF16)
    k_rot_low = _rope(proj(wkpe_ref, 0, KPE_PAD), cc, sa, sb)
    k_rot = (k_rot_low.astype(BF16), pltpu.roll(k_rot_low, B_ROPE_DIM, 1).astype(BF16))
    ckvn = _rms(proj(wmid_ref, OFF_CKV - OFF_AZ, KV_LORA_RANK), gkv_ref[...]).astype(BF16)
    kv_w = B_NOPE_DIM + B_V_DIM
    for h in range(B_HEADS):
        kv = jnp.dot(ckvn, wkv_ref[:, h * kv_w:(h + 1) * kv_w], preferred_element_type=F32)
        kf_ref[0, h, :, :B_NOPE_DIM] = kv[:, :B_NOPE_DIM].astype(BF16)
        kf_ref[0, h, :, B_NOPE_DIM:] = k_rot[h % 2]
        v_ref[0, h, :, :] = kv[:, B_NOPE_DIM:].astype(BF16)


def _mla_proj(h, w_mid, w_bz, w_kpe, g_q, g_kv, wq_nope, wq_pe, wkv, cc, sa, sb, bsz, seq):
    n_tok, d = h.shape
    tm = ROW_TILE
    per_b = seq // tm
    row = lambda i: (i, 0)
    full = lambda i: (0, 0)
    pos = lambda i: (i % per_b, 0)
    head_major = lambda i: (i // per_b, 0, i % per_b, 0)
    resident = lambda w: pl.BlockSpec(w.shape, full, pipeline_mode=pl.Buffered(1))
    return pl.pallas_call(
        _mla_proj_kernel,
        grid=(n_tok // tm,),
        in_specs=[
            pl.BlockSpec((tm, d), row),
            resident(w_mid), resident(w_bz), resident(w_kpe),
            pl.BlockSpec((1, Q_LORA_RANK), full),
            pl.BlockSpec((1, KV_LORA_RANK), full),
            resident(wq_nope), resident(wq_pe), resident(wkv),
            pl.BlockSpec((tm, LANES), pos),
            pl.BlockSpec((tm, LANES), pos),
            pl.BlockSpec((tm, LANES), pos),
        ],
        out_specs=[
            pl.BlockSpec((tm, A_WIDTH), row),
            pl.BlockSpec((tm, B_WIDTH), row),
            pl.BlockSpec((1, B_HEADS, tm, B_QK_PAD), head_major),
            pl.BlockSpec((1, B_HEADS, tm, B_QK_PAD), head_major),
            pl.BlockSpec((1, B_HEADS, tm, B_V_DIM), head_major),
        ],
        out_shape=[
            jax.ShapeDtypeStruct((n_tok, A_WIDTH), BF16),
            jax.ShapeDtypeStruct((n_tok, B_WIDTH), BF16),
            jax.ShapeDtypeStruct((bsz, B_HEADS, seq, B_QK_PAD), BF16),
            jax.ShapeDtypeStruct((bsz, B_HEADS, seq, B_QK_PAD), BF16),
            jax.ShapeDtypeStruct((bsz, B_HEADS, seq, B_V_DIM), BF16),
        ],
        compiler_params=_params(1),
        name="mla_proj",
    )(h, w_mid, w_bz, w_kpe, g_q.reshape(1, -1), g_kv.reshape(1, -1), wq_nope, wq_pe, wkv,
      cc, sa, sb)


def _mla_attn_kernel(q_ref, k_ref, v_ref, zg_ref, y_ref, vext_ref, s_ref):
    heads = q_ref.shape[1]
    seq = k_ref.shape[2]
    n_blk = seq // MLA_Q
    for hh in range(heads):
        vext_ref[hh, :, :B_V_DIM] = v_ref[0, hh, :, :]
        vext_ref[hh, :, B_V_DIM:] = jnp.ones((seq, B_V_DIM), BF16)

    def scores(hh, blk, slot):
        s_ref[slot] = _dot_nt(q_ref[0, hh, pl.ds(blk * MLA_Q, MLA_Q), :], k_ref[0, hh, :, :])

    def softmax_pv(hh, blk, slot):
        s = s_ref[slot]
        m = jnp.max(s, axis=-1, keepdims=True)
        p = jnp.exp2(s - m).astype(BF16)
        ol = jnp.dot(p, vext_ref[hh], preferred_element_type=F32)
        o = ol[:, :B_V_DIM] / ol[:, B_V_DIM:]
        rows = pl.ds(blk * MLA_Q, MLA_Q)
        cols = slice(hh * B_V_DIM, (hh + 1) * B_V_DIM)
        y_ref[0, rows, cols] = (o * zg_ref[0, rows, cols].astype(F32)).astype(BF16)

    items = [(hh, blk) for hh in range(heads) for blk in range(n_blk)]
    scores(*items[0], 0)
    for n, item in enumerate(items):
        if n + 1 < len(items):
            scores(*items[n + 1], (n + 1) % 2)
        softmax_pv(*item, n % 2)


def _mla_attn(qf, kf, v, bzg3):
    bsz, heads, seq, _ = qf.shape
    grp = MLA_HEAD_GROUP
    head_blk = lambda b, g: (b, g, 0, 0)
    tok_spec = pl.BlockSpec((1, seq, grp * B_V_DIM), lambda b, g: (b, 0, g))
    return pl.pallas_call(
        _mla_attn_kernel,
        grid=(bsz, heads // grp),
        in_specs=[
            pl.BlockSpec((1, grp, seq, B_QK_PAD), head_blk),
            pl.BlockSpec((1, grp, seq, B_QK_PAD), head_blk),
            pl.BlockSpec((1, grp, seq, B_V_DIM), head_blk),
            tok_spec,
        ],
        out_specs=tok_spec,
        out_shape=jax.ShapeDtypeStruct((bsz, seq, heads * B_V_DIM), BF16),
        scratch_shapes=[pltpu.VMEM((grp, seq, 2 * B_V_DIM), BF16), pltpu.VMEM((2, MLA_Q, seq), F32)],
        compiler_params=_params(2),
        name="mla_attn",
    )(qf, kf, v, bzg3)


def _out_kernel(ya_ref, yb_ref, w_ref, x_ref, mod_ref, g_ref, o_ref, y_ref):
    tm, d = y_ref.shape
    chunk = PROJ_DOT_COLS
    ssq = jnp.zeros((tm, LANES), F32)
    for c0 in range(0, d, chunk):
        y = (jnp.dot(ya_ref[...], w_ref[:A_WIDTH, c0:c0 + chunk], preferred_element_type=F32)
             + jnp.dot(yb_ref[...], w_ref[A_WIDTH:, c0:c0 + chunk], preferred_element_type=F32))
        y_ref[:, c0:c0 + chunk] = y
        for t0 in range(0, chunk, LANES):
            ssq = ssq + y[:, t0:t0 + LANES] * y[:, t0:t0 + LANES]
    inv = lax.rsqrt(jnp.sum(ssq, axis=-1, keepdims=True) * (1.0 / d) + NORM_EPS)
    gain = mod_ref[0, 2:3, :] * g_ref[...]
    o_ref[...] = x_ref[...] + y_ref[...] * inv * gain


def _out_proj(ya, yb, w_out, x2, mod3, g_post, seq):
    n_tok, d = x2.shape
    tm = ROW_TILE
    per_b = seq // tm
    row = lambda i: (i, 0)
    return pl.pallas_call(
        _out_kernel,
        grid=(n_tok // tm,),
        in_specs=[
            pl.BlockSpec((tm, A_WIDTH), row),
            pl.BlockSpec((tm, B_WIDTH), row),
            pl.BlockSpec(w_out.shape, lambda i: (0, 0), pipeline_mode=pl.Buffered(1)),
            pl.BlockSpec((tm, d), row),
            pl.BlockSpec((1, 3, d), lambda i: (i // per_b, 0, 0)),
            pl.BlockSpec((1, d), lambda i: (0, 0)),
        ],
        out_specs=pl.BlockSpec((tm, d), row),
        out_shape=jax.ShapeDtypeStruct((n_tok, d), F32),
        scratch_shapes=[pltpu.VMEM((tm, d), F32)],
        compiler_params=_params(1),
        name="out_proj",
    )(ya, yb, w_out, x2, mod3, g_post.reshape(1, d))


def _split_w_uq(w_uq):
    r = w_uq.shape[0]
    w = w_uq.reshape(r, B_HEADS, B_NOPE_DIM + B_ROPE_DIM).astype(BF16)
    return (w[:, :, :B_NOPE_DIM].reshape(r, B_HEADS * B_NOPE_DIM),
            w[:, :, B_NOPE_DIM:].reshape(r, B_HEADS * B_ROPE_DIM))


def _rope_tables(seq):
    half = B_ROPE_DIM // 2
    pos = jnp.arange(seq, dtype=F32)
    inv_freq = jnp.power(ROPE_THETA, -jnp.arange(half, dtype=F32) / half)
    ang = pos[:, None] * inv_freq[None, :]
    cos, sin = jnp.cos(ang), jnp.sin(ang)
    z = jnp.zeros_like(cos)
    cc = jnp.concatenate([cos, cos, cos, cos], axis=1)
    sa = jnp.concatenate([-sin, z, -sin, z], axis=1)
    sb = jnp.concatenate([z, sin, z, sin], axis=1)
    return cc, sa, sb


def _layer(x, c, w_ada, b_ada, g_pre, w_in, g_q_lora, w_uq, g_kv_lora, w_ukv, w_out, g_post):
    bsz, seq, d = x.shape
    x2 = x.reshape(bsz * seq, d)
    mod3 = _ada(c, w_ada, b_ada).reshape(bsz, 3, d)
    w_qkv, w_mid, w_bz, w_kpe = _cast_w_in(w_in.T)
    h, s1, s4, s16 = _qkv_proj(x2, mod3, g_pre, w_qkv, bsz, seq)
    cc, sa, sb = _rope_tables(seq)
    zg, bzg, qf, kf, v = _mla_proj(h, w_mid, w_bz, w_kpe, g_q_lora, g_kv_lora, *_split_w_uq(w_uq),
                                   w_ukv.astype(BF16), cc, sa, sb, bsz, seq)
    slopes = jnp.exp2(-8.0 * jnp.arange(1, A_HEADS + 1, dtype=F32) / A_HEADS)
    ya = _dilated((s1, s4, s16), zg.reshape(bsz, seq, A_WIDTH), slopes)
    yb = _mla_attn(qf, kf, v, bzg.reshape(bsz, seq, B_WIDTH))
    out = _out_proj(ya.reshape(bsz * seq, A_WIDTH), yb.reshape(bsz * seq, B_WIDTH),
                    w_out.astype(BF16), x2, mod3, g_post, seq)
    return out.reshape(bsz, seq, d)


def kernel(x, c, w_ada, b_ada, g_pre, w_in, g_q_lora, w_uq, g_kv_lora, w_ukv, w_out, g_post):
    for layer in range(w_ada.shape[0]):
        x = _layer(x, c, w_ada[layer], b_ada[layer], g_pre[layer], w_in[layer], g_q_lora[layer],
                   w_uq[layer], g_kv_lora[layer], w_ukv[layer], w_out[layer], g_post[layer])
    return x
```

```python
import jax
import jax.numpy as jnp
import numpy as np
from jax import lax
from jax.experimental import pallas as pl
from jax.experimental.pallas import tpu as pltpu

F32 = jnp.float32
BF16 = jnp.bfloat16

D_MODEL = 2048
A_HEADS = 8
A_HEAD_DIM = 128
A_WIDTH = A_HEADS * A_HEAD_DIM
DILATED_PATTERNS = ((128, 1), (512, 4), (2048, 16))
B_HEADS = 8
B_V_DIM = 128
B_WIDTH = B_HEADS * B_V_DIM
B_NOPE_DIM = 128
B_ROPE_DIM = 64
Q_LORA_RANK = 512
KV_LORA_RANK = 256
ROPE_THETA = 10000.0
NORM_EPS = 1e-6
NEG_INF = -1e30
LOG2E = 1.4426950408889634

LANES = 128
B_QK_PAD = 256
KPE_PAD = LANES

OFF_AZ = 3 * A_WIDTH
OFF_CQ = OFF_AZ + A_WIDTH
OFF_CKV = OFF_CQ + Q_LORA_RANK

ADA_COL_TILE = 1024
CAST_COL_TILE = 512
ROW_TILE = 512
QKV_HEADS_PER_DOT = 8
PROJ_DOT_COLS = 1024
BAND_Q = 128
BAND_HEAD_GROUP = 2
MLA_Q = 256
MLA_HEAD_GROUP = 2
VMEM_LIMIT = 56 * 1024 * 1024


def _params(grid_rank):
    return pltpu.CompilerParams(dimension_semantics=("arbitrary",) * grid_rank,
                                vmem_limit_bytes=VMEM_LIMIT)


def _silu(v):
    return v * (1.0 / (1.0 + jnp.exp(-v)))


def _rms(v, g):
    return v * lax.rsqrt(jnp.mean(v * v, axis=-1, keepdims=True) + NORM_EPS) * g


def _dot_nt(a, b):
    return lax.dot_general(a, b, (((1,), (1,)), ((), ())), preferred_element_type=F32)


def _ada_kernel(c_ref, w_ref, b_ref, o_ref):
    s = _silu(c_ref[...]).astype(BF16)
    o_ref[...] = jnp.dot(s, w_ref[...].astype(BF16), preferred_element_type=F32) + b_ref[...]


def _ada(c, w_ada, b_ada):
    bsz, d = c.shape
    n = w_ada.shape[1]
    tn = ADA_COL_TILE
    return pl.pallas_call(
        _ada_kernel,
        grid=(n // tn,),
        in_specs=[
            pl.BlockSpec((bsz, d), lambda j: (0, 0)),
            pl.BlockSpec((d, tn), lambda j: (0, j)),
            pl.BlockSpec((1, tn), lambda j: (0, j)),
        ],
        out_specs=pl.BlockSpec((bsz, tn), lambda j: (0, j)),
        out_shape=jax.ShapeDtypeStruct((bsz, n), F32),
        compiler_params=_params(1),
        name="ada_mod",
    )(c, w_ada, b_ada.reshape(1, n))


def _cast_w_in_kernel(w_ref, qkv_ref, mid_ref, bz_ref, kpe_ref):
    kpe0 = OFF_CKV + KV_LORA_RANK
    qkv_ref[...] = w_ref[:OFF_AZ, :].astype(BF16)
    mid_ref[...] = w_ref[OFF_AZ:kpe0, :].astype(BF16)
    bz_ref[...] = w_ref[kpe0 + B_ROPE_DIM:, :].astype(BF16)
    kpe_ref[:B_ROPE_DIM, :] = w_ref[kpe0:kpe0 + B_ROPE_DIM, :].astype(BF16)
    kpe_ref[B_ROPE_DIM:, :] = jnp.zeros((KPE_PAD - B_ROPE_DIM, w_ref.shape[1]), BF16)


def _cast_w_in(w_in_t):
    n, d = w_in_t.shape
    tc = CAST_COL_TILE
    heights = (OFF_AZ, OFF_CKV + KV_LORA_RANK - OFF_AZ, B_WIDTH, KPE_PAD)
    return pl.pallas_call(
        _cast_w_in_kernel,
        grid=(d // tc,),
        in_specs=[pl.BlockSpec((n, tc), lambda j: (0, j))],
        out_specs=[pl.BlockSpec((hgt, tc), lambda j: (0, j)) for hgt in heights],
        out_shape=[jax.ShapeDtypeStruct((hgt, d), BF16) for hgt in heights],
        compiler_params=_params(1),
        name="cast_w_in",
    )(w_in_t)


def _qkv_kernel(x_ref, mod_ref, g_ref, w_ref, h_ref, s1_ref, s4_ref, s16_ref, acc_ref, acc4_ref):
    tm = x_ref.shape[0]
    d4 = DILATED_PATTERNS[1][1]
    x = x_ref[...]
    gain = g_ref[...] * (1.0 + mod_ref[0, 1:2, :])
    inv = lax.rsqrt(jnp.mean(x * x, axis=-1, keepdims=True) + NORM_EPS)
    h_ref[...] = (x * inv * gain + mod_ref[0, 0:1, :]).astype(BF16)

    heads_per = acc_ref.shape[0]
    chunk = heads_per * A_HEAD_DIM
    q_scale = A_HEAD_DIM ** -0.5 * LOG2E
    for t in range(3):
        for c0 in range(0, A_WIDTH, chunk):
            acc = _dot_nt(h_ref[...], w_ref[t * A_WIDTH + c0:t * A_WIDTH + c0 + chunk, :])
            if t == 0:
                acc = acc * q_scale
            for k in range(heads_per):
                head = c0 // A_HEAD_DIM + k
                a = acc[:, k * A_HEAD_DIM:(k + 1) * A_HEAD_DIM]
                s1_ref[t, 0, head, :, :] = a.astype(BF16)
                acc_ref[k, :, :] = a
            for k in range(heads_per):
                head = c0 // A_HEAD_DIM + k
                for r4 in range(d4):
                    g4 = acc_ref[k, pl.ds(r4, tm // d4, stride=d4), :]
                    s4_ref[t, 0, head, :, r4 * LANES:(r4 + 1) * LANES] = g4.astype(BF16)
                    acc4_ref[k, r4, :, :] = g4
                for r4 in range(d4):
                    for a4 in range(d4):
                        r = r4 + d4 * a4
                        s16_ref[t, 0, head, :, r * LANES:(r + 1) * LANES] = (
                            acc4_ref[k, r4, pl.ds(a4, tm // (d4 * d4), stride=d4), :].astype(BF16))


def _qkv_proj(x2, mod3, g_pre, w_qkv, bsz, seq):
    n_tok, d = x2.shape
    tm = ROW_TILE
    per_b = seq // tm
    hd = A_HEAD_DIM
    d4, d16 = DILATED_PATTERNS[1][1], DILATED_PATTERNS[2][1]
    assert d16 == d4 * d4
    head_major = lambda i: (0, i // per_b, 0, i % per_b, 0)
    return pl.pallas_call(
        _qkv_kernel,
        grid=(n_tok // tm,),
        in_specs=[
            pl.BlockSpec((tm, d), lambda i: (i, 0)),
            pl.BlockSpec((1, 3, d), lambda i: (i // per_b, 0, 0)),
            pl.BlockSpec((1, d), lambda i: (0, 0)),
            pl.BlockSpec(w_qkv.shape, lambda i: (0, 0), pipeline_mode=pl.Buffered(1)),
        ],
        out_specs=[
            pl.BlockSpec((tm, d), lambda i: (i, 0)),
            pl.BlockSpec((3, 1, A_HEADS, tm, hd), head_major),
            pl.BlockSpec((3, 1, A_HEADS, tm // d4, d4 * hd), head_major),
            pl.BlockSpec((3, 1, A_HEADS, tm // d16, d16 * hd), head_major),
        ],
        out_shape=[
            jax.ShapeDtypeStruct((n_tok, d), BF16),
            jax.ShapeDtypeStruct((3, bsz, A_HEADS, seq, hd), BF16),
            jax.ShapeDtypeStruct((3, bsz, A_HEADS, seq // d4, d4 * hd), BF16),
            jax.ShapeDtypeStruct((3, bsz, A_HEADS, seq // d16, d16 * hd), BF16),
        ],
        scratch_shapes=[pltpu.VMEM((QKV_HEADS_PER_DOT, tm, hd), F32),
                        pltpu.VMEM((QKV_HEADS_PER_DOT, d4, tm // d4, hd), F32)],
        compiler_params=_params(1),
        name="qkv_proj",
    )(x2, mod3, g_pre.reshape(1, d), w_qkv)


def _band_tables():
    bq, win = BAND_Q, 2 * BAND_Q
    radius = DILATED_PATTERNS[0][0] // 2 // DILATED_PATTERNS[0][1]
    rows = np.arange(bq)[:, None]
    cols = np.arange(win)[None, :]
    tabs = []
    for k_minus_q in (0, -radius, -bq, 0):
        dist = np.abs(cols + k_minus_q - rows)
        tabs.append(np.where(dist <= radius, -LOG2E * dist, NEG_INF))
    tabs[3][:, bq:] = NEG_INF
    return np.stack(tabs).astype(np.float32)


def _band_block(q, k_win, v_win, bias):
    s = _dot_nt(q, k_win) + bias
    m = jnp.max(s, axis=-1, keepdims=True)
    p = jnp.exp2(s - m).astype(BF16)
    v_ext = jnp.concatenate([v_win, jnp.ones_like(v_win)], axis=1)
    ol = jnp.dot(p, v_ext, preferred_element_type=F32)
    return ol[:, :A_HEAD_DIM], m, ol[:, A_HEAD_DIM:]


def _dil_kernel(slope_ref, tab_ref, s1_ref, s4_ref, s16_ref, zg_ref, y_ref,
                acc_ref, m_ref, l_ref, stage_ref):
    heads = s1_ref.shape[2]
    seq = s1_ref.shape[3]
    bq = BAND_Q
    win = 2 * bq
    views = (s1_ref, s4_ref, s16_ref)
    stats = (acc_ref, m_ref, l_ref)
    d4 = DILATED_PATTERNS[1][1]

    def blocks_of(hh, p_idx):
        window, dil = DILATED_PATTERNS[p_idx]
        radius = window // 2 // dil
        length = seq // dil
        ref = views[p_idx]
        pen = slope_ref[pl.program_id(1) * heads + hh] * float(dil)
        n_blk = length // bq
        def scaled(tab):
            return jnp.where(tab > 0.5 * NEG_INF, pen * tab, NEG_INF)

        bias = ([scaled(tab_ref[3, :, :bq])] if n_blk == 1 else
                [scaled(tab_ref[kind]) for kind in range(3)])
        for r in sorted(range(dil), key=lambda r: (r % d4, r)):
            cols = slice(r * LANES, (r + 1) * LANES)
            if n_blk == 1:
                yield 0, r, _band_block(ref[0, 0, hh, :, cols], ref[1, 0, hh, :, cols],
                                        ref[2, 0, hh, :, cols], bias[0])
                continue
            for qi in range(n_blk):
                kind = 0 if qi == 0 else (2 if qi == n_blk - 1 else 1)
                q_lo = qi * bq
                k_lo = min(max(q_lo - radius, 0), length - win)
                yield q_lo, r, _band_block(ref[0, 0, hh, pl.ds(q_lo, bq), cols],
                                           ref[1, 0, hh, pl.ds(k_lo, win), cols],
                                           ref[2, 0, hh, pl.ds(k_lo, win), cols], bias[kind])

    def strided_phase(hh):
        for p_idx in range(1, len(DILATED_PATTERNS)):
            dil = DILATED_PATTERNS[p_idx][1]
            for q_lo, r, (acc, m, l) in blocks_of(hh, p_idx):
                vals = (acc, jnp.broadcast_to(m, acc.shape), l)
                if dil == d4:
                    rows = pl.ds(q_lo * dil + r, bq, stride=dil)
                    for ref, val in zip(stats, vals):
                        ref[hh, p_idx - 1, rows, :] = val
                else:
                    r4, a4 = r % d4, r // d4
                    for k, val in enumerate(vals):
                        stage_ref[hh, k, r4, pl.ds(a4, bq, stride=d4), :] = val
                    if a4 == d4 - 1:
                        for k, ref in enumerate(stats):
                            ref[hh, p_idx - 1, pl.ds(r4, seq // d4, stride=d4), :] = (
                                stage_ref[hh, k, r4, :, :])
                yield

    def mix_phase(hh):
        out_cols = slice(hh * A_HEAD_DIM, (hh + 1) * A_HEAD_DIM)
        for q_lo, _, (acc0, m0, l0) in blocks_of(hh, 0):
            rs = pl.ds(q_lo, bq)
            m1, m2 = m_ref[hh, 0, rs, :], m_ref[hh, 1, rs, :]
            m = jnp.maximum(jnp.maximum(m0, m1), m2)
            e0, e1, e2 = jnp.exp2(m0 - m), jnp.exp2(m1 - m), jnp.exp2(m2 - m)
            num = e0 * acc0 + e1 * acc_ref[hh, 0, rs, :] + e2 * acc_ref[hh, 1, rs, :]
            den = e0 * l0 + e1 * l_ref[hh, 0, rs, :] + e2 * l_ref[hh, 1, rs, :]
            y_ref[0, rs, out_cols] = (num / den * zg_ref[0, rs, out_cols].astype(F32)).astype(BF16)
            yield

    prev_mix = None
    for hh in range(heads):
        cur = strided_phase(hh)
        if prev_mix is None:
            for _ in cur:
                pass
        else:
            for _ in cur:
                next(prev_mix, None)
                next(cur, None)
            for _ in prev_mix:
                pass
        prev_mix = mix_phase(hh)
    for _ in prev_mix:
        pass


def _dilated(views, zg3, slopes):
    _, bsz, heads, seq, hd = views[0].shape
    assert DILATED_PATTERNS[0][1] == 1 and len(views) == len(DILATED_PATTERNS)
    grp = BAND_HEAD_GROUP
    d4 = DILATED_PATTERNS[1][1]
    assert [p[1] for p in DILATED_PATTERNS] == [1, d4, d4 * d4]
    tables = _band_tables()
    specs = [pl.BlockSpec((3, 1, grp) + v.shape[3:], lambda b, g: (0, b, g, 0, 0)) for v in views]
    tok_spec = pl.BlockSpec((1, seq, grp * hd), lambda b, g: (b, 0, g))
    n_strided = len(DILATED_PATTERNS) - 1
    return pl.pallas_call(
        _dil_kernel,
        grid=(bsz, heads // grp),
        in_specs=[pl.BlockSpec(memory_space=pltpu.SMEM),
                  pl.BlockSpec(tables.shape, lambda b, g: (0, 0, 0))] + specs + [tok_spec],
        out_specs=tok_spec,
        out_shape=jax.ShapeDtypeStruct((bsz, seq, heads * hd), BF16),
        scratch_shapes=([pltpu.VMEM((grp, n_strided, seq, hd), F32) for _ in range(3)]
                        + [pltpu.VMEM((grp, 3, d4, seq // d4, hd), F32)]),
        compiler_params=_params(2),
        name="dilated_mix",
    )(slopes, jnp.asarray(tables), *views, zg3)


def _rope(pe, cc, sa, sb):
    return pe * cc + pltpu.roll(pe, 96, 1) * sa + pltpu.roll(pe, 32, 1) * sb


def _mla_proj_kernel(h_ref, wmid_ref, wbz_ref, wkpe_ref, gq_ref, gkv_ref, wqn_ref, wqp_ref, wkv_ref,
                     cc_ref, sa_ref, sb_ref, zg_ref, bzg_ref, qf_ref, kf_ref, v_ref):
    def proj(w_ref, off, width):
        return _dot_nt(h_ref[...], w_ref[off:off + width, :])

    chunk = PROJ_DOT_COLS
    for c0 in range(0, A_WIDTH, chunk):
        zg_ref[:, c0:c0 + chunk] = _silu(proj(wmid_ref, c0, chunk)).astype(BF16)
    for c0 in range(0, B_WIDTH, chunk):
        bzg_ref[:, c0:c0 + chunk] = _silu(proj(wbz_ref, c0, chunk)).astype(BF16)

    cc, sa, sb = cc_ref[...], sa_ref[...], sb_ref[...]
    q_scale = (B_NOPE_DIM + B_ROPE_DIM) ** -0.5 * LOG2E
    cqn = _rms(proj(wmid_ref, OFF_CQ - OFF_AZ, Q_LORA_RANK), gq_ref[...]).astype(BF16)
    low_half = lax.broadcasted_iota(jnp.int32, (1, LANES), 1) < B_ROPE_DIM
    q_pe = jnp.dot(cqn, wqp_ref[...], preferred_element_type=F32)
    for h0 in range(0, B_HEADS, 2):
        q_nope = jnp.dot(cqn, wqn_ref[:, h0 * B_NOPE_DIM:(h0 + 2) * B_NOPE_DIM],
                         preferred_element_type=F32) * q_scale
        q_rot = _rope(q_pe[:, h0 * B_ROPE_DIM:(h0 + 2) * B_ROPE_DIM], cc, sa, sb) * q_scale
        for h in (h0, h0 + 1):
            own = low_half if h == h0 else jnp.logical_not(low_half)
            qf_ref[0, h, :, :B_NOPE_DIM] = (
                q_nope[:, (h - h0) * B_NOPE_DIM:(h - h0 + 1) * B_NOPE_DIM].astype(BF16))
            qf_ref[0, h, :, B_NOPE_DIM:] = jnp.where(own, q_rot, 0.0).astype(BF16)
    k_rot_low = _rope(proj(wkpe_ref, 0, KPE_PAD), cc, sa, sb)
    k_rot = (k_rot_low.astype(BF16), pltpu.roll(k_rot_low, B_ROPE_DIM, 1).astype(BF16))
    ckvn = _rms(proj(wmid_ref, OFF_CKV - OFF_AZ, KV_LORA_RANK), gkv_ref[...]).astype(BF16)
    kv_w = B_NOPE_DIM + B_V_DIM
    for h in range(B_HEADS):
        kv = jnp.dot(ckvn, wkv_ref[:, h * kv_w:(h + 1) * kv_w], preferred_element_type=F32)
        kf_ref[0, h, :, :B_NOPE_DIM] = kv[:, :B_NOPE_DIM].astype(BF16)
        kf_ref[0, h, :, B_NOPE_DIM:] = k_rot[h % 2]
        v_ref[0, h, :, :] = kv[:, B_NOPE_DIM:].astype(BF16)


def _mla_proj(h, w_mid, w_bz, w_kpe, g_q, g_kv, wq_nope, wq_pe, wkv, cc, sa, sb, bsz, seq):
    n_tok, d = h.shape
    tm = ROW_TILE
    per_b = seq // tm
    row = lambda i: (i, 0)
    full = lambda i: (0, 0)
    pos = lambda i: (i % per_b, 0)
    head_major = lambda i: (i // per_b, 0, i % per_b, 0)
    resident = lambda w: pl.BlockSpec(w.shape, full, pipeline_mode=pl.Buffered(1))
    return pl.pallas_call(
        _mla_proj_kernel,
        grid=(n_tok // tm,),
        in_specs=[
            pl.BlockSpec((tm, d), row),
            resident(w_mid), resident(w_bz), resident(w_kpe),
            pl.BlockSpec((1, Q_LORA_RANK), full),
            pl.BlockSpec((1, KV_LORA_RANK), full),
            resident(wq_nope), resident(wq_pe), resident(wkv),
            pl.BlockSpec((tm, LANES), pos),
            pl.BlockSpec((tm, LANES), pos),
            pl.BlockSpec((tm, LANES), pos),
        ],
        out_specs=[
            pl.BlockSpec((tm, A_WIDTH), row),
            pl.BlockSpec((tm, B_WIDTH), row),
            pl.BlockSpec((1, B_HEADS, tm, B_QK_PAD), head_major),
            pl.BlockSpec((1, B_HEADS, tm, B_QK_PAD), head_major),
            pl.BlockSpec((1, B_HEADS, tm, B_V_DIM), head_major),
        ],
        out_shape=[
            jax.ShapeDtypeStruct((n_tok, A_WIDTH), BF16),
            jax.ShapeDtypeStruct((n_tok, B_WIDTH), BF16),
            jax.ShapeDtypeStruct((bsz, B_HEADS, seq, B_QK_PAD), BF16),
            jax.ShapeDtypeStruct((bsz, B_HEADS, seq, B_QK_PAD), BF16),
            jax.ShapeDtypeStruct((bsz, B_HEADS, seq, B_V_DIM), BF16),
        ],
        compiler_params=_params(1),
        name="mla_proj",
    )(h, w_mid, w_bz, w_kpe, g_q.reshape(1, -1), g_kv.reshape(1, -1), wq_nope, wq_pe, wkv,
      cc, sa, sb)


def _mla_attn_kernel(q_ref, k_ref, v_ref, zg_ref, y_ref, vext_ref, s_ref):
    heads = q_ref.shape[1]
    seq = k_ref.shape[2]
    n_blk = seq // MLA_Q
    for hh in range(heads):
        vext_ref[hh, :, :B_V_DIM] = v_ref[0, hh, :, :]
        vext_ref[hh, :, B_V_DIM:] = jnp.ones((seq, B_V_DIM), BF16)

    def scores(hh, blk, slot):
        s_ref[slot] = _dot_nt(q_ref[0, hh, pl.ds(blk * MLA_Q, MLA_Q), :], k_ref[0, hh, :, :])

    def softmax_pv(hh, blk, slot):
        s = s_ref[slot]
        m = jnp.max(s, axis=-1, keepdims=True)
        p = jnp.exp2(s - m).astype(BF16)
        ol = jnp.dot(p, vext_ref[hh], preferred_element_type=F32)
        o = ol[:, :B_V_DIM] / ol[:, B_V_DIM:]
        rows = pl.ds(blk * MLA_Q, MLA_Q)
        cols = slice(hh * B_V_DIM, (hh + 1) * B_V_DIM)
        y_ref[0, rows, cols] = (o * zg_ref[0, rows, cols].astype(F32)).astype(BF16)

    items = [(hh, blk) for hh in range(heads) for blk in range(n_blk)]
    scores(*items[0], 0)
    for n, item in enumerate(items):
        if n + 1 < len(items):
            scores(*items[n + 1], (n + 1) % 2)
        softmax_pv(*item, n % 2)


def _mla_attn(qf, kf, v, bzg3):
    bsz, heads, seq, _ = qf.shape
    grp = MLA_HEAD_GROUP
    head_blk = lambda b, g: (b, g, 0, 0)
    tok_spec = pl.BlockSpec((1, seq, grp * B_V_DIM), lambda b, g: (b, 0, g))
    return pl.pallas_call(
        _mla_attn_kernel,
        grid=(bsz, heads // grp),
        in_specs=[
            pl.BlockSpec((1, grp, seq, B_QK_PAD), head_blk),
            pl.BlockSpec((1, grp, seq, B_QK_PAD), head_blk),
            pl.BlockSpec((1, grp, seq, B_V_DIM), head_blk),
            tok_spec,
        ],
        out_specs=tok_spec,
        out_shape=jax.ShapeDtypeStruct((bsz, seq, heads * B_V_DIM), BF16),
        scratch_shapes=[pltpu.VMEM((grp, seq, 2 * B_V_DIM), BF16), pltpu.VMEM((2, MLA_Q, seq), F32)],
        compiler_params=_params(2),
        name="mla_attn",
    )(qf, kf, v, bzg3)


def _out_kernel(ya_ref, yb_ref, w_ref, x_ref, mod_ref, g_ref, o_ref, y_ref):
    tm, d = y_ref.shape
    chunk = PROJ_DOT_COLS
    ssq = jnp.zeros((tm, LANES), F32)
    for c0 in range(0, d, chunk):
        y = (jnp.dot(ya_ref[...], w_ref[:A_WIDTH, c0:c0 + chunk], preferred_element_type=F32)
             + jnp.dot(yb_ref[...], w_ref[A_WIDTH:, c0:c0 + chunk], preferred_element_type=F32))
        y_ref[:, c0:c0 + chunk] = y
        for t0 in range(0, chunk, LANES):
            ssq = ssq + y[:, t0:t0 + LANES] * y[:, t0:t0 + LANES]
    inv = lax.rsqrt(jnp.sum(ssq, axis=-1, keepdims=True) * (1.0 / d) + NORM_EPS)
    gain = mod_ref[0, 2:3, :] * g_ref[...]
    o_ref[...] = x_ref[...] + y_ref[...] * inv * gain


def _out_proj(ya, yb, w_out, x2, mod3, g_post, seq):
    n_tok, d = x2.shape
    tm = ROW_TILE
    per_b = seq // tm
    row = lambda i: (i, 0)
    return pl.pallas_call(
        _out_kernel,
        grid=(n_tok // tm,),
        in_specs=[
            pl.BlockSpec((tm, A_WIDTH), row),
            pl.BlockSpec((tm, B_WIDTH), row),
            pl.BlockSpec(w_out.shape, lambda i: (0, 0), pipeline_mode=pl.Buffered(1)),
            pl.BlockSpec((tm, d), row),
            pl.BlockSpec((1, 3, d), lambda i: (i // per_b, 0, 0)),
            pl.BlockSpec((1, d), lambda i: (0, 0)),
        ],
        out_specs=pl.BlockSpec((tm, d), row),
        out_shape=jax.ShapeDtypeStruct((n_tok, d), F32),
        scratch_shapes=[pltpu.VMEM((tm, d), F32)],
        compiler_params=_params(1),
        name="out_proj",
    )(ya, yb, w_out, x2, mod3, g_post.reshape(1, d))


def _split_w_uq(w_uq):
    r = w_uq.shape[0]
    w = w_uq.reshape(r, B_HEADS, B_NOPE_DIM + B_ROPE_DIM).astype(BF16)
    return (w[:, :, :B_NOPE_DIM].reshape(r, B_HEADS * B_NOPE_DIM),
            w[:, :, B_NOPE_DIM:].reshape(r, B_HEADS * B_ROPE_DIM))


def _rope_tables(seq):
    half = B_ROPE_DIM // 2
    inv_freq = np.power(ROPE_THETA, -np.arange(half, dtype=np.float64) / half)
    ang = np.arange(seq, dtype=np.float64)[:, None] * inv_freq[None, :]
    cos, sin = np.cos(ang), np.sin(ang)
    z = np.zeros_like(cos)
    cc = np.concatenate([cos, cos, cos, cos], axis=1)
    sa = np.concatenate([-sin, z, -sin, z], axis=1)
    sb = np.concatenate([z, sin, z, sin], axis=1)
    return tuple(jnp.asarray(t.astype(np.float32)) for t in (cc, sa, sb))


def _layer(x, c, w_ada, b_ada, g_pre, w_in, g_q_lora, w_uq, g_kv_lora, w_ukv, w_out, g_post):
    bsz, seq, d = x.shape
    x2 = x.reshape(bsz * seq, d)
    mod3 = _ada(c, w_ada, b_ada).reshape(bsz, 3, d)
    w_qkv, w_mid, w_bz, w_kpe = _cast_w_in(w_in.T)
    h, s1, s4, s16 = _qkv_proj(x2, mod3, g_pre, w_qkv, bsz, seq)
    cc, sa, sb = _rope_tables(seq)
    zg, bzg, qf, kf, v = _mla_proj(h, w_mid, w_bz, w_kpe, g_q_lora, g_kv_lora, *_split_w_uq(w_uq),
                                   w_ukv.astype(BF16), cc, sa, sb, bsz, seq)
    slopes = jnp.asarray(np.exp2(-8.0 * np.arange(1, A_HEADS + 1) / A_HEADS).astype(np.float32))
    ya = _dilated((s1, s4, s16), zg.reshape(bsz, seq, A_WIDTH), slopes)
    yb = _mla_attn(qf, kf, v, bzg.reshape(bsz, seq, B_WIDTH))
    out = _out_proj(ya.reshape(bsz * seq, A_WIDTH), yb.reshape(bsz * seq, B_WIDTH),
                    w_out.astype(BF16), x2, mod3, g_post, seq)
    return out.reshape(bsz, seq, d)


def kernel(x, c, w_ada, b_ada, g_pre, w_in, g_q_lora, w_uq, g_kv_lora, w_ukv, w_out, g_post):
    for layer in range(w_ada.shape[0]):
        x = _layer(x, c, w_ada[layer], b_ada[layer], g_pre[layer], w_in[layer], g_q_lora[layer],
                   w_uq[layer], g_kv_lora[layer], w_ukv[layer], w_out[layer], g_post[layer])
    return x
```

```python
import jax
import jax.numpy as jnp
import numpy as np
from jax import lax
from jax.experimental import pallas as pl
from jax.experimental.pallas import tpu as pltpu

F32 = jnp.float32
BF16 = jnp.bfloat16

D_MODEL = 2048
A_HEADS = 8
A_HEAD_DIM = 128
A_WIDTH = A_HEADS * A_HEAD_DIM
DILATED_PATTERNS = ((128, 1), (512, 4), (2048, 16))
B_HEADS = 8
B_V_DIM = 128
B_WIDTH = B_HEADS * B_V_DIM
B_NOPE_DIM = 128
B_ROPE_DIM = 64
Q_LORA_RANK = 512
KV_LORA_RANK = 256
ROPE_THETA = 10000.0
NORM_EPS = 1e-6
NEG_INF = -1e30
LOG2E = 1.4426950408889634

LANES = 128
B_QK_PAD = 256
KPE_PAD = LANES

OFF_AZ = 3 * A_WIDTH
OFF_CQ = OFF_AZ + A_WIDTH
OFF_CKV = OFF_CQ + Q_LORA_RANK

ADA_COL_TILE = 1024
CAST_COL_TILE = 512
ROW_TILE = 512
QKV_HEADS_PER_DOT = 8
PROJ_DOT_COLS = 1024
BAND_Q = 128
BAND_HEAD_GROUP = 2
MLA_Q = 256
MLA_HEAD_GROUP = 2
VMEM_LIMIT = 56 * 1024 * 1024


def _params(grid_rank):
    return pltpu.CompilerParams(dimension_semantics=("arbitrary",) * grid_rank,
                                vmem_limit_bytes=VMEM_LIMIT)


def _silu(v):
    return v * (1.0 / (1.0 + jnp.exp(-v)))


def _rms(v, g):
    return v * lax.rsqrt(jnp.mean(v * v, axis=-1, keepdims=True) + NORM_EPS) * g


def _dot_nt(a, b):
    return lax.dot_general(a, b, (((1,), (1,)), ((), ())), preferred_element_type=F32)


def _ada_kernel(c_ref, w_ref, b_ref, o_ref):
    s = _silu(c_ref[...]).astype(BF16)
    o_ref[...] = jnp.dot(s, w_ref[...].astype(BF16), preferred_element_type=F32) + b_ref[...]


def _ada(c, w_ada, b_ada):
    bsz, d = c.shape
    n = w_ada.shape[1]
    tn = ADA_COL_TILE
    return pl.pallas_call(
        _ada_kernel,
        grid=(n // tn,),
        in_specs=[
            pl.BlockSpec((bsz, d), lambda j: (0, 0)),
            pl.BlockSpec((d, tn), lambda j: (0, j)),
            pl.BlockSpec((1, tn), lambda j: (0, j)),
        ],
        out_specs=pl.BlockSpec((bsz, tn), lambda j: (0, j)),
        out_shape=jax.ShapeDtypeStruct((bsz, n), F32),
        compiler_params=_params(1),
        name="ada_mod",
    )(c, w_ada, b_ada.reshape(1, n))


def _cast_w_in_kernel(w_ref, qkv_ref, mid_ref, bz_ref, kpe_ref):
    kpe0 = OFF_CKV + KV_LORA_RANK
    qkv_ref[...] = w_ref[:OFF_AZ, :].astype(BF16)
    mid_ref[...] = w_ref[OFF_AZ:kpe0, :].astype(BF16)
    bz_ref[...] = w_ref[kpe0 + B_ROPE_DIM:, :].astype(BF16)
    kpe_ref[:B_ROPE_DIM, :] = w_ref[kpe0:kpe0 + B_ROPE_DIM, :].astype(BF16)
    kpe_ref[B_ROPE_DIM:, :] = jnp.zeros((KPE_PAD - B_ROPE_DIM, w_ref.shape[1]), BF16)


def _cast_w_in(w_in_t):
    n, d = w_in_t.shape
    tc = CAST_COL_TILE
    heights = (OFF_AZ, OFF_CKV + KV_LORA_RANK - OFF_AZ, B_WIDTH, KPE_PAD)
    return pl.pallas_call(
        _cast_w_in_kernel,
        grid=(d // tc,),
        in_specs=[pl.BlockSpec((n, tc), lambda j: (0, j))],
        out_specs=[pl.BlockSpec((hgt, tc), lambda j: (0, j)) for hgt in heights],
        out_shape=[jax.ShapeDtypeStruct((hgt, d), BF16) for hgt in heights],
        compiler_params=_params(1),
        name="cast_w_in",
    )(w_in_t)


def _qkv_kernel(x_ref, mod_ref, g_ref, w_ref, h_ref, s1_ref, s4_ref, s16_ref, acc_ref, acc4_ref):
    tm = x_ref.shape[0]
    d4 = DILATED_PATTERNS[1][1]
    x = x_ref[...]
    gain = g_ref[...] * (1.0 + mod_ref[0, 1:2, :])
    inv = lax.rsqrt(jnp.mean(x * x, axis=-1, keepdims=True) + NORM_EPS)
    h_ref[...] = (x * inv * gain + mod_ref[0, 0:1, :]).astype(BF16)

    heads_per = acc_ref.shape[0]
    chunk = heads_per * A_HEAD_DIM
    q_scale = A_HEAD_DIM ** -0.5 * LOG2E
    for t in range(3):
        for c0 in range(0, A_WIDTH, chunk):
            acc = _dot_nt(h_ref[...], w_ref[t * A_WIDTH + c0:t * A_WIDTH + c0 + chunk, :])
            if t == 0:
                acc = acc * q_scale
            for k in range(heads_per):
                head = c0 // A_HEAD_DIM + k
                a = acc[:, k * A_HEAD_DIM:(k + 1) * A_HEAD_DIM]
                s1_ref[t, 0, head, :, :] = a.astype(BF16)
                acc_ref[k, :, :] = a
            for k in range(heads_per):
                head = c0 // A_HEAD_DIM + k
                for r4 in range(d4):
                    g4 = acc_ref[k, pl.ds(r4, tm // d4, stride=d4), :]
                    s4_ref[t, 0, head, :, r4 * LANES:(r4 + 1) * LANES] = g4.astype(BF16)
                    acc4_ref[k, r4, :, :] = g4
                for r4 in range(d4):
                    for a4 in range(d4):
                        r = r4 + d4 * a4
                        s16_ref[t, 0, head, :, r * LANES:(r + 1) * LANES] = (
                            acc4_ref[k, r4, pl.ds(a4, tm // (d4 * d4), stride=d4), :].astype(BF16))


def _qkv_proj(x2, mod3, g_pre, w_qkv, bsz, seq):
    n_tok, d = x2.shape
    tm = ROW_TILE
    per_b = seq // tm
    hd = A_HEAD_DIM
    d4, d16 = DILATED_PATTERNS[1][1], DILATED_PATTERNS[2][1]
    assert d16 == d4 * d4
    head_major = lambda i: (0, i // per_b, 0, i % per_b, 0)
    return pl.pallas_call(
        _qkv_kernel,
        grid=(n_tok // tm,),
        in_specs=[
            pl.BlockSpec((tm, d), lambda i: (i, 0)),
            pl.BlockSpec((1, 3, d), lambda i: (i // per_b, 0, 0)),
            pl.BlockSpec((1, d), lambda i: (0, 0)),
            pl.BlockSpec(w_qkv.shape, lambda i: (0, 0), pipeline_mode=pl.Buffered(1)),
        ],
        out_specs=[
            pl.BlockSpec((tm, d), lambda i: (i, 0)),
            pl.BlockSpec((3, 1, A_HEADS, tm, hd), head_major),
            pl.BlockSpec((3, 1, A_HEADS, tm // d4, d4 * hd), head_major),
            pl.BlockSpec((3, 1, A_HEADS, tm // d16, d16 * hd), head_major),
        ],
        out_shape=[
            jax.ShapeDtypeStruct((n_tok, d), BF16),
            jax.ShapeDtypeStruct((3, bsz, A_HEADS, seq, hd), BF16),
            jax.ShapeDtypeStruct((3, bsz, A_HEADS, seq // d4, d4 * hd), BF16),
            jax.ShapeDtypeStruct((3, bsz, A_HEADS, seq // d16, d16 * hd), BF16),
        ],
        scratch_shapes=[pltpu.VMEM((QKV_HEADS_PER_DOT, tm, hd), F32),
                        pltpu.VMEM((QKV_HEADS_PER_DOT, d4, tm // d4, hd), F32)],
        compiler_params=_params(1),
        name="qkv_proj",
    )(x2, mod3, g_pre.reshape(1, d), w_qkv)


def _band_tables():
    bq, win = BAND_Q, 2 * BAND_Q
    radius = DILATED_PATTERNS[0][0] // 2 // DILATED_PATTERNS[0][1]
    rows = np.arange(bq)[:, None]
    cols = np.arange(win)[None, :]
    tabs = []
    for k_minus_q in (0, -radius, -bq, 0):
        dist = np.abs(cols + k_minus_q - rows)
        tabs.append(np.where(dist <= radius, -LOG2E * dist, NEG_INF))
    tabs[3][:, bq:] = NEG_INF
    return np.stack(tabs).astype(np.float32)


def _band_block(q, k_win, v_win, bias):
    s = _dot_nt(q, k_win) + bias
    m = jnp.max(s, axis=-1, keepdims=True)
    p = jnp.exp2(s - m).astype(BF16)
    v_ext = jnp.concatenate([v_win, jnp.ones_like(v_win)], axis=1)
    ol = jnp.dot(p, v_ext, preferred_element_type=F32)
    return ol[:, :A_HEAD_DIM], m, ol[:, A_HEAD_DIM:]


def _dil_kernel(slope_ref, tab_ref, s1_ref, s4_ref, s16_ref, zg_ref, y_ref,
                acc_ref, m_ref, l_ref, stage_ref):
    heads = s1_ref.shape[2]
    seq = s1_ref.shape[3]
    bq = BAND_Q
    win = 2 * bq
    views = (s1_ref, s4_ref, s16_ref)
    stats = (acc_ref, m_ref, l_ref)
    d4 = DILATED_PATTERNS[1][1]

    def blocks_of(hh, p_idx):
        window, dil = DILATED_PATTERNS[p_idx]
        radius = window // 2 // dil
        length = seq // dil
        ref = views[p_idx]
        pen = slope_ref[pl.program_id(1) * heads + hh] * float(dil)
        n_blk = length // bq
        def scaled(tab):
            return jnp.where(tab > 0.5 * NEG_INF, pen * tab, NEG_INF)

        bias = ([scaled(tab_ref[3, :, :bq])] if n_blk == 1 else
                [scaled(tab_ref[kind]) for kind in range(3)])
        for r in sorted(range(dil), key=lambda r: (r % d4, r)):
            cols = slice(r * LANES, (r + 1) * LANES)
            if n_blk == 1:
                yield 0, r, _band_block(ref[0, 0, hh, :, cols], ref[1, 0, hh, :, cols],
                                        ref[2, 0, hh, :, cols], bias[0])
                continue
            for qi in range(n_blk):
                kind = 0 if qi == 0 else (2 if qi == n_blk - 1 else 1)
                q_lo = qi * bq
                k_lo = min(max(q_lo - radius, 0), length - win)
                yield q_lo, r, _band_block(ref[0, 0, hh, pl.ds(q_lo, bq), cols],
                                           ref[1, 0, hh, pl.ds(k_lo, win), cols],
                                           ref[2, 0, hh, pl.ds(k_lo, win), cols], bias[kind])

    def strided_phase(hh):
        for p_idx in range(1, len(DILATED_PATTERNS)):
            dil = DILATED_PATTERNS[p_idx][1]
            for q_lo, r, (acc, m, l) in blocks_of(hh, p_idx):
                vals = (acc, jnp.broadcast_to(m, acc.shape), l)
                if dil == d4:
                    rows = pl.ds(q_lo * dil + r, bq, stride=dil)
                    for ref, val in zip(stats, vals):
                        ref[hh, p_idx - 1, rows, :] = val
                else:
                    r4, a4 = r % d4, r // d4
                    for k, val in enumerate(vals):
                        stage_ref[hh, k, r4, pl.ds(a4, bq, stride=d4), :] = val
                    if a4 == d4 - 1:
                        for k, ref in enumerate(stats):
                            ref[hh, p_idx - 1, pl.ds(r4, seq // d4, stride=d4), :] = (
                                stage_ref[hh, k, r4, :, :])
                yield

    def mix_phase(hh):
        out_cols = slice(hh * A_HEAD_DIM, (hh + 1) * A_HEAD_DIM)
        for q_lo, _, (acc0, m0, l0) in blocks_of(hh, 0):
            rs = pl.ds(q_lo, bq)
            m1, m2 = m_ref[hh, 0, rs, :], m_ref[hh, 1, rs, :]
            m = jnp.maximum(jnp.maximum(m0, m1), m2)
            e0, e1, e2 = jnp.exp2(m0 - m), jnp.exp2(m1 - m), jnp.exp2(m2 - m)
            num = e0 * acc0 + e1 * acc_ref[hh, 0, rs, :] + e2 * acc_ref[hh, 1, rs, :]
            den = e0 * l0 + e1 * l_ref[hh, 0, rs, :] + e2 * l_ref[hh, 1, rs, :]
            y_ref[0, rs, out_cols] = (num / den * zg_ref[0, rs, out_cols].astype(F32)).astype(BF16)
            yield

    prev_mix = None
    for hh in range(heads):
        cur = strided_phase(hh)
        if prev_mix is None:
            for _ in cur:
                pass
        else:
            for _ in cur:
                next(prev_mix, None)
                next(cur, None)
            for _ in prev_mix:
                pass
        prev_mix = mix_phase(hh)
    for _ in prev_mix:
        pass


def _dilated(views, zg3, slopes):
    _, bsz, heads, seq, hd = views[0].shape
    assert DILATED_PATTERNS[0][1] == 1 and len(views) == len(DILATED_PATTERNS)
    grp = BAND_HEAD_GROUP
    d4 = DILATED_PATTERNS[1][1]
    assert [p[1] for p in DILATED_PATTERNS] == [1, d4, d4 * d4]
    tables = _band_tables()
    specs = [pl.BlockSpec((3, 1, grp) + v.shape[3:], lambda b, g: (0, b, g, 0, 0)) for v in views]
    tok_spec = pl.BlockSpec((1, seq, grp * hd), lambda b, g: (b, 0, g))
    n_strided = len(DILATED_PATTERNS) - 1
    return pl.pallas_call(
        _dil_kernel,
        grid=(bsz, heads // grp),
        in_specs=[pl.BlockSpec(memory_space=pltpu.SMEM),
                  pl.BlockSpec(tables.shape, lambda b, g: (0, 0, 0))] + specs + [tok_spec],
        out_specs=tok_spec,
        out_shape=jax.ShapeDtypeStruct((bsz, seq, heads * hd), BF16),
        scratch_shapes=([pltpu.VMEM((grp, n_strided, seq, hd), F32) for _ in range(3)]
                        + [pltpu.VMEM((grp, 3, d4, seq // d4, hd), F32)]),
        compiler_params=_params(2),
        name="dilated_mix",
    )(slopes, jnp.asarray(tables), *views, zg3)


def _rope(pe, cc, sa, sb):
    return pe * cc + pltpu.roll(pe, 96, 1) * sa + pltpu.roll(pe, 32, 1) * sb


def _mla_proj_kernel(h_ref, wmid_ref, wbz_ref, wkpe_ref, gq_ref, gkv_ref, wqn_ref, wqp_ref, wkv_ref,
                     cc_ref, sa_ref, sb_ref, zg_ref, bzg_ref, qf_ref, kf_ref, v_ref):
    def proj(w_ref, off, width):
        return _dot_nt(h_ref[...], w_ref[off:off + width, :])

    chunk = PROJ_DOT_COLS
    for c0 in range(0, A_WIDTH, chunk):
        zg_ref[:, c0:c0 + chunk] = _silu(proj(wmid_ref, c0, chunk)).astype(BF16)
    for c0 in range(0, B_WIDTH, chunk):
        bzg_ref[:, c0:c0 + chunk] = _silu(proj(wbz_ref, c0, chunk)).astype(BF16)

    cc, sa, sb = cc_ref[...], sa_ref[...], sb_ref[...]
    q_scale = (B_NOPE_DIM + B_ROPE_DIM) ** -0.5 * LOG2E
    cqn = _rms(proj(wmid_ref, OFF_CQ - OFF_AZ, Q_LORA_RANK), gq_ref[...]).astype(BF16)
    low_half = lax.broadcasted_iota(jnp.int32, (1, LANES), 1) < B_ROPE_DIM
    q_pe = jnp.dot(cqn, wqp_ref[...], preferred_element_type=F32)
    for h0 in range(0, B_HEADS, 2):
        q_nope = jnp.dot(cqn, wqn_ref[:, h0 * B_NOPE_DIM:(h0 + 2) * B_NOPE_DIM],
                         preferred_element_type=F32) * q_scale
        q_rot = _rope(q_pe[:, h0 * B_ROPE_DIM:(h0 + 2) * B_ROPE_DIM], cc, sa, sb) * q_scale
        for h in (h0, h0 + 1):
            own = low_half if h == h0 else jnp.logical_not(low_half)
            qf_ref[0, h, :, :B_NOPE_DIM] = (
                q_nope[:, (h - h0) * B_NOPE_DIM:(h - h0 + 1) * B_NOPE_DIM].astype(BF16))
            qf_ref[0, h, :, B_NOPE_DIM:] = jnp.where(own, q_rot, 0.0).astype(BF16)
    k_rot_low = _rope(proj(wkpe_ref, 0, KPE_PAD), cc, sa, sb)
    k_rot = (k_rot_low.astype(BF16), pltpu.roll(k_rot_low, B_ROPE_DIM, 1).astype(BF16))
    ckvn = _rms(proj(wmid_ref, OFF_CKV - OFF_AZ, KV_LORA_RANK), gkv_ref[...]).astype(BF16)
    kv_w = B_NOPE_DIM + B_V_DIM
    for h in range(B_HEADS):
        kv = jnp.dot(ckvn, wkv_ref[:, h * kv_w:(h + 1) * kv_w], preferred_element_type=F32)
        kf_ref[0, h, :, :B_NOPE_DIM] = kv[:, :B_NOPE_DIM].astype(BF16)
        kf_ref[0, h, :, B_NOPE_DIM:] = k_rot[h % 2]
        v_ref[0, h, :, :] = kv[:, B_NOPE_DIM:].astype(BF16)


def _mla_proj(h, w_mid, w_bz, w_kpe, g_q, g_kv, wq_nope, wq_pe, wkv, cc, sa, sb, bsz, seq):
    n_tok, d = h.shape
    tm = ROW_TILE
    per_b = seq // tm
    row = lambda i: (i, 0)
    full = lambda i: (0, 0)
    pos = lambda i: (i % per_b, 0)
    head_major = lambda i: (i // per_b, 0, i % per_b, 0)
    resident = lambda w: pl.BlockSpec(w.shape, full, pipeline_mode=pl.Buffered(1))
    return pl.pallas_call(
        _mla_proj_kernel,
        grid=(n_tok // tm,),
        in_specs=[
            pl.BlockSpec((tm, d), row),
            resident(w_mid), resident(w_bz), resident(w_kpe),
            pl.BlockSpec((1, Q_LORA_RANK), full),
            pl.BlockSpec((1, KV_LORA_RANK), full),
            resident(wq_nope), resident(wq_pe), resident(wkv),
            pl.BlockSpec((tm, LANES), pos),
            pl.BlockSpec((tm, LANES), pos),
            pl.BlockSpec((tm, LANES), pos),
        ],
        out_specs=[
            pl.BlockSpec((tm, A_WIDTH), row),
            pl.BlockSpec((tm, B_WIDTH), row),
            pl.BlockSpec((1, B_HEADS, tm, B_QK_PAD), head_major),
            pl.BlockSpec((1, B_HEADS, tm, B_QK_PAD), head_major),
            pl.BlockSpec((1, B_HEADS, tm, B_V_DIM), head_major),
        ],
        out_shape=[
            jax.ShapeDtypeStruct((n_tok, A_WIDTH), BF16),
            jax.ShapeDtypeStruct((n_tok, B_WIDTH), BF16),
            jax.ShapeDtypeStruct((bsz, B_HEADS, seq, B_QK_PAD), BF16),
            jax.ShapeDtypeStruct((bsz, B_HEADS, seq, B_QK_PAD), BF16),
            jax.ShapeDtypeStruct((bsz, B_HEADS, seq, B_V_DIM), BF16),
        ],
        compiler_params=_params(1),
        name="mla_proj",
    )(h, w_mid, w_bz, w_kpe, g_q.reshape(1, -1), g_kv.reshape(1, -1), wq_nope, wq_pe, wkv,
      cc, sa, sb)


def _mla_attn_kernel(q_ref, k_ref, v_ref, zg_ref, y_ref, vext_ref, s_ref):
    heads = q_ref.shape[1]
    seq = k_ref.shape[2]
    n_blk = seq // MLA_Q
    for hh in range(heads):
        vext_ref[hh, :, :B_V_DIM] = v_ref[0, hh, :, :]
        vext_ref[hh, :, B_V_DIM:] = jnp.ones((seq, B_V_DIM), BF16)

    def scores(hh, blk, slot):
        s_ref[slot] = _dot_nt(q_ref[0, hh, pl.ds(blk * MLA_Q, MLA_Q), :], k_ref[0, hh, :, :])

    def softmax_pv(hh, blk, slot):
        s = s_ref[slot]
        m = jnp.max(s, axis=-1, keepdims=True)
        p = jnp.exp2(s - m).astype(BF16)
        ol = jnp.dot(p, vext_ref[hh], preferred_element_type=F32)
        o = ol[:, :B_V_DIM] / ol[:, B_V_DIM:]
        rows = pl.ds(blk * MLA_Q, MLA_Q)
        cols = slice(hh * B_V_DIM, (hh + 1) * B_V_DIM)
        y_ref[0, rows, cols] = (o * zg_ref[0, rows, cols].astype(F32)).astype(BF16)

    items = [(hh, blk) for hh in range(heads) for blk in range(n_blk)]
    scores(*items[0], 0)
    for n, item in enumerate(items):
        if n + 1 < len(items):
            scores(*items[n + 1], (n + 1) % 2)
        softmax_pv(*item, n % 2)


def _mla_attn(qf, kf, v, bzg3):
    bsz, heads, seq, _ = qf.shape
    grp = MLA_HEAD_GROUP
    head_blk = lambda b, g: (b, g, 0, 0)
    tok_spec = pl.BlockSpec((1, seq, grp * B_V_DIM), lambda b, g: (b, 0, g))
    return pl.pallas_call(
        _mla_attn_kernel,
        grid=(bsz, heads // grp),
        in_specs=[
            pl.BlockSpec((1, grp, seq, B_QK_PAD), head_blk),
            pl.BlockSpec((1, grp, seq, B_QK_PAD), head_blk),
            pl.BlockSpec((1, grp, seq, B_V_DIM), head_blk),
            tok_spec,
        ],
        out_specs=tok_spec,
        out_shape=jax.ShapeDtypeStruct((bsz, seq, heads * B_V_DIM), BF16),
        scratch_shapes=[pltpu.VMEM((grp, seq, 2 * B_V_DIM), BF16), pltpu.VMEM((2, MLA_Q, seq), F32)],
        compiler_params=_params(2),
        name="mla_attn",
    )(qf, kf, v, bzg3)


def _out_kernel(ya_ref, yb_ref, w32_ref, x_ref, mod_ref, g_ref, o_ref, y_ref, w_ref):
    tm, d = y_ref.shape
    chunk = PROJ_DOT_COLS

    @pl.when(pl.program_id(0) == 0)
    def _():
        for r0 in range(0, w_ref.shape[0], 256):
            w_ref[r0:r0 + 256, :] = w32_ref[r0:r0 + 256, :].astype(BF16)

    ssq = jnp.zeros((tm, LANES), F32)
    for c0 in range(0, d, chunk):
        y = (jnp.dot(ya_ref[...], w_ref[:A_WIDTH, c0:c0 + chunk], preferred_element_type=F32)
             + jnp.dot(yb_ref[...], w_ref[A_WIDTH:, c0:c0 + chunk], preferred_element_type=F32))
        y_ref[:, c0:c0 + chunk] = y
        for t0 in range(0, chunk, LANES):
            ssq = ssq + y[:, t0:t0 + LANES] * y[:, t0:t0 + LANES]
    inv = lax.rsqrt(jnp.sum(ssq, axis=-1, keepdims=True) * (1.0 / d) + NORM_EPS)
    gain = mod_ref[0, 2:3, :] * g_ref[...]
    o_ref[...] = x_ref[...] + y_ref[...] * inv * gain


def _out_proj(ya, yb, w_out, x2, mod3, g_post, seq):
    n_tok, d = x2.shape
    tm = ROW_TILE
    per_b = seq // tm
    row = lambda i: (i, 0)
    return pl.pallas_call(
        _out_kernel,
        grid=(n_tok // tm,),
        in_specs=[
            pl.BlockSpec((tm, A_WIDTH), row),
            pl.BlockSpec((tm, B_WIDTH), row),
            pl.BlockSpec(w_out.shape, lambda i: (0, 0), pipeline_mode=pl.Buffered(1)),
            pl.BlockSpec((tm, d), row),
            pl.BlockSpec((1, 3, d), lambda i: (i // per_b, 0, 0)),
            pl.BlockSpec((1, d), lambda i: (0, 0)),
        ],
        out_specs=pl.BlockSpec((tm, d), row),
        out_shape=jax.ShapeDtypeStruct((n_tok, d), F32),
        scratch_shapes=[pltpu.VMEM((tm, d), F32), pltpu.VMEM(w_out.shape, BF16)],
        compiler_params=_params(1),
        name="out_proj",
    )(ya, yb, w_out, x2, mod3, g_post.reshape(1, d))


def _split_w_uq(w_uq):
    r = w_uq.shape[0]
    w = w_uq.reshape(r, B_HEADS, B_NOPE_DIM + B_ROPE_DIM).astype(BF16)
    return (w[:, :, :B_NOPE_DIM].reshape(r, B_HEADS * B_NOPE_DIM),
            w[:, :, B_NOPE_DIM:].reshape(r, B_HEADS * B_ROPE_DIM))


def _rope_tables(seq):
    half = B_ROPE_DIM // 2
    inv_freq = np.power(ROPE_THETA, -np.arange(half, dtype=np.float64) / half)
    ang = np.arange(seq, dtype=np.float64)[:, None] * inv_freq[None, :]
    cos, sin = np.cos(ang), np.sin(ang)
    z = np.zeros_like(cos)
    cc = np.concatenate([cos, cos, cos, cos], axis=1)
    sa = np.concatenate([-sin, z, -sin, z], axis=1)
    sb = np.concatenate([z, sin, z, sin], axis=1)
    return tuple(jnp.asarray(t.astype(np.float32)) for t in (cc, sa, sb))


def _layer(x, c, w_ada, b_ada, g_pre, w_in, g_q_lora, w_uq, g_kv_lora, w_ukv, w_out, g_post):
    bsz, seq, d = x.shape
    x2 = x.reshape(bsz * seq, d)
    mod3 = _ada(c, w_ada, b_ada).reshape(bsz, 3, d)
    w_qkv, w_mid, w_bz, w_kpe = _cast_w_in(w_in.T)
    h, s1, s4, s16 = _qkv_proj(x2, mod3, g_pre, w_qkv, bsz, seq)
    cc, sa, sb = _rope_tables(seq)
    zg, bzg, qf, kf, v = _mla_proj(h, w_mid, w_bz, w_kpe, g_q_lora, g_kv_lora, *_split_w_uq(w_uq),
                                   w_ukv.astype(BF16), cc, sa, sb, bsz, seq)
    slopes = jnp.asarray(np.exp2(-8.0 * np.arange(1, A_HEADS + 1) / A_HEADS).astype(np.float32))
    ya = _dilated((s1, s4, s16), zg.reshape(bsz, seq, A_WIDTH), slopes)
    yb = _mla_attn(qf, kf, v, bzg.reshape(bsz, seq, B_WIDTH))
    out = _out_proj(ya.reshape(bsz * seq, A_WIDTH), yb.reshape(bsz * seq, B_WIDTH),
                    w_out, x2, mod3, g_post, seq)
    return out.reshape(bsz, seq, d)


def kernel(x, c, w_ada, b_ada, g_pre, w_in, g_q_lora, w_uq, g_kv_lora, w_ukv, w_out, g_post):
    for layer in range(w_ada.shape[0]):
        x = _layer(x, c, w_ada[layer], b_ada[layer], g_pre[layer], w_in[layer], g_q_lora[layer],
                   w_uq[layer], g_kv_lora[layer], w_ukv[layer], w_out[layer], g_post[layer])
    return x
```

```python
import jax
import jax.numpy as jnp
import numpy as np
from jax import lax
from jax.experimental import pallas as pl
from jax.experimental.pallas import tpu as pltpu

F32 = jnp.float32
BF16 = jnp.bfloat16

D_MODEL = 2048
A_HEADS = 8
A_HEAD_DIM = 128
A_WIDTH = A_HEADS * A_HEAD_DIM
DILATED_PATTERNS = ((128, 1), (512, 4), (2048, 16))
B_HEADS = 8
B_V_DIM = 128
B_WIDTH = B_HEADS * B_V_DIM
B_NOPE_DIM = 128
B_ROPE_DIM = 64
Q_LORA_RANK = 512
KV_LORA_RANK = 256
ROPE_THETA = 10000.0
NORM_EPS = 1e-6
NEG_INF = -1e30
LOG2E = 1.4426950408889634

LANES = 128
B_QK_PAD = 256
KPE_PAD = LANES

OFF_AZ = 3 * A_WIDTH
OFF_CQ = OFF_AZ + A_WIDTH
OFF_CKV = OFF_CQ + Q_LORA_RANK

ADA_COL_TILE = 1024
CAST_COL_TILE = 512
ROW_TILE = 512
QKV_HEADS_PER_DOT = 8
PROJ_DOT_COLS = 1024
BAND_Q = 128
BAND_HEAD_GROUP = 2
MLA_Q = 256
MLA_HEAD_GROUP = 2
VMEM_LIMIT = 56 * 1024 * 1024


def _params(grid_rank):
    return pltpu.CompilerParams(dimension_semantics=("arbitrary",) * grid_rank,
                                vmem_limit_bytes=VMEM_LIMIT)


def _silu(v):
    return v * (1.0 / (1.0 + jnp.exp(-v)))


def _rms(v, g):
    return v * lax.rsqrt(jnp.mean(v * v, axis=-1, keepdims=True) + NORM_EPS) * g


def _dot_nt(a, b):
    return lax.dot_general(a, b, (((1,), (1,)), ((), ())), preferred_element_type=F32)


def _ada_kernel(c_ref, w_ref, b_ref, o_ref):
    s = _silu(c_ref[...]).astype(BF16)
    o_ref[...] = jnp.dot(s, w_ref[...].astype(BF16), preferred_element_type=F32) + b_ref[...]


def _ada(c, w_ada, b_ada):
    bsz, d = c.shape
    n = w_ada.shape[1]
    tn = ADA_COL_TILE
    return pl.pallas_call(
        _ada_kernel,
        grid=(n // tn,),
        in_specs=[
            pl.BlockSpec((bsz, d), lambda j: (0, 0)),
            pl.BlockSpec((d, tn), lambda j: (0, j)),
            pl.BlockSpec((1, tn), lambda j: (0, j)),
        ],
        out_specs=pl.BlockSpec((bsz, tn), lambda j: (0, j)),
        out_shape=jax.ShapeDtypeStruct((bsz, n), F32),
        compiler_params=_params(1),
        name="ada_mod",
    )(c, w_ada, b_ada.reshape(1, n))


def _cast_w_in_kernel(w_ref, qkv_ref, mid_ref, bz_ref, kpe_ref):
    kpe0 = OFF_CKV + KV_LORA_RANK
    qkv_ref[...] = w_ref[:OFF_AZ, :].astype(BF16)
    mid_ref[...] = w_ref[OFF_AZ:kpe0, :].astype(BF16)
    bz_ref[...] = w_ref[kpe0 + B_ROPE_DIM:, :].astype(BF16)
    kpe_ref[:B_ROPE_DIM, :] = w_ref[kpe0:kpe0 + B_ROPE_DIM, :].astype(BF16)
    kpe_ref[B_ROPE_DIM:, :] = jnp.zeros((KPE_PAD - B_ROPE_DIM, w_ref.shape[1]), BF16)


def _cast_w_in(w_in_t):
    n, d = w_in_t.shape
    tc = CAST_COL_TILE
    heights = (OFF_AZ, OFF_CKV + KV_LORA_RANK - OFF_AZ, B_WIDTH, KPE_PAD)
    return pl.pallas_call(
        _cast_w_in_kernel,
        grid=(d // tc,),
        in_specs=[pl.BlockSpec((n, tc), lambda j: (0, j))],
        out_specs=[pl.BlockSpec((hgt, tc), lambda j: (0, j)) for hgt in heights],
        out_shape=[jax.ShapeDtypeStruct((hgt, d), BF16) for hgt in heights],
        compiler_params=_params(1),
        name="cast_w_in",
    )(w_in_t)


def _qkv_kernel(x_ref, mod_ref, g_ref, w_ref, h_ref, s1_ref, s4_ref, s16_ref, acc_ref, acc4_ref):
    tm = x_ref.shape[0]
    d4 = DILATED_PATTERNS[1][1]
    x = x_ref[...]
    gain = g_ref[...] * (1.0 + mod_ref[0, 1:2, :])
    inv = lax.rsqrt(jnp.mean(x * x, axis=-1, keepdims=True) + NORM_EPS)
    h_ref[...] = (x * inv * gain + mod_ref[0, 0:1, :]).astype(BF16)

    heads_per = acc_ref.shape[0]
    chunk = heads_per * A_HEAD_DIM
    q_scale = A_HEAD_DIM ** -0.5 * LOG2E
    for t in range(3):
        for c0 in range(0, A_WIDTH, chunk):
            acc = _dot_nt(h_ref[...], w_ref[t * A_WIDTH + c0:t * A_WIDTH + c0 + chunk, :])
            if t == 0:
                acc = acc * q_scale
            for k in range(heads_per):
                head = c0 // A_HEAD_DIM + k
                a = acc[:, k * A_HEAD_DIM:(k + 1) * A_HEAD_DIM]
                s1_ref[t, 0, head, :, :] = a.astype(BF16)
                acc_ref[k, :, :] = a
            for k in range(heads_per):
                head = c0 // A_HEAD_DIM + k
                for r4 in range(d4):
                    g4 = acc_ref[k, pl.ds(r4, tm // d4, stride=d4), :]
                    s4_ref[t, 0, head, :, r4 * LANES:(r4 + 1) * LANES] = g4.astype(BF16)
                    acc4_ref[k, r4, :, :] = g4
                for r4 in range(d4):
                    for a4 in range(d4):
                        r = r4 + d4 * a4
                        s16_ref[t, 0, head, :, r * LANES:(r + 1) * LANES] = (
                            acc4_ref[k, r4, pl.ds(a4, tm // (d4 * d4), stride=d4), :].astype(BF16))


def _qkv_proj(x2, mod3, g_pre, w_qkv, bsz, seq):
    n_tok, d = x2.shape
    tm = ROW_TILE
    per_b = seq // tm
    hd = A_HEAD_DIM
    d4, d16 = DILATED_PATTERNS[1][1], DILATED_PATTERNS[2][1]
    assert d16 == d4 * d4
    head_major = lambda i: (0, i // per_b, 0, i % per_b, 0)
    return pl.pallas_call(
        _qkv_kernel,
        grid=(n_tok // tm,),
        in_specs=[
            pl.BlockSpec((tm, d), lambda i: (i, 0)),
            pl.BlockSpec((1, 3, d), lambda i: (i // per_b, 0, 0)),
            pl.BlockSpec((1, d), lambda i: (0, 0)),
            pl.BlockSpec(w_qkv.shape, lambda i: (0, 0), pipeline_mode=pl.Buffered(1)),
        ],
        out_specs=[
            pl.BlockSpec((tm, d), lambda i: (i, 0)),
            pl.BlockSpec((3, 1, A_HEADS, tm, hd), head_major),
            pl.BlockSpec((3, 1, A_HEADS, tm // d4, d4 * hd), head_major),
            pl.BlockSpec((3, 1, A_HEADS, tm // d16, d16 * hd), head_major),
        ],
        out_shape=[
            jax.ShapeDtypeStruct((n_tok, d), BF16),
            jax.ShapeDtypeStruct((3, bsz, A_HEADS, seq, hd), BF16),
            jax.ShapeDtypeStruct((3, bsz, A_HEADS, seq // d4, d4 * hd), BF16),
            jax.ShapeDtypeStruct((3, bsz, A_HEADS, seq // d16, d16 * hd), BF16),
        ],
        scratch_shapes=[pltpu.VMEM((QKV_HEADS_PER_DOT, tm, hd), F32),
                        pltpu.VMEM((QKV_HEADS_PER_DOT, d4, tm // d4, hd), F32)],
        compiler_params=_params(1),
        name="qkv_proj",
    )(x2, mod3, g_pre.reshape(1, d), w_qkv)


def _band_tables():
    bq, win = BAND_Q, 2 * BAND_Q
    radius = DILATED_PATTERNS[0][0] // 2 // DILATED_PATTERNS[0][1]
    rows = np.arange(bq)[:, None]
    cols = np.arange(win)[None, :]
    tabs = []
    for k_minus_q in (0, -radius, -bq, 0):
        dist = np.abs(cols + k_minus_q - rows)
        tabs.append(np.where(dist <= radius, -LOG2E * dist, NEG_INF))
    tabs[3][:, bq:] = NEG_INF
    return np.stack(tabs).astype(np.float32)


def _band_block(q, k_win, v_win, bias):
    s = _dot_nt(q, k_win) + bias
    m = jnp.max(s, axis=-1, keepdims=True)
    p = jnp.exp2(s - m).astype(BF16)
    v_ext = jnp.concatenate([v_win, jnp.ones_like(v_win)], axis=1)
    ol = jnp.dot(p, v_ext, preferred_element_type=F32)
    return ol[:, :A_HEAD_DIM], m, ol[:, A_HEAD_DIM:]


def _dil_kernel(slope_ref, tab_ref, s1_ref, s4_ref, s16_ref, zg_ref, y_ref,
                acc_ref, m_ref, l_ref, stage_ref):
    heads = s1_ref.shape[2]
    seq = s1_ref.shape[3]
    bq = BAND_Q
    win = 2 * bq
    views = (s1_ref, s4_ref, s16_ref)
    stats = (acc_ref, m_ref, l_ref)
    d4 = DILATED_PATTERNS[1][1]

    def blocks_of(hh, p_idx):
        window, dil = DILATED_PATTERNS[p_idx]
        radius = window // 2 // dil
        length = seq // dil
        ref = views[p_idx]
        pen = slope_ref[pl.program_id(1) * heads + hh] * float(dil)
        n_blk = length // bq
        def scaled(tab):
            return jnp.where(tab > 0.5 * NEG_INF, pen * tab, NEG_INF)

        bias = ([scaled(tab_ref[3, :, :bq])] if n_blk == 1 else
                [scaled(tab_ref[kind]) for kind in range(3)])
        for r in sorted(range(dil), key=lambda r: (r % d4, r)):
            cols = slice(r * LANES, (r + 1) * LANES)
            if n_blk == 1:
                yield 0, r, _band_block(ref[0, 0, hh, :, cols], ref[1, 0, hh, :, cols],
                                        ref[2, 0, hh, :, cols], bias[0])
                continue
            for qi in range(n_blk):
                kind = 0 if qi == 0 else (2 if qi == n_blk - 1 else 1)
                q_lo = qi * bq
                k_lo = min(max(q_lo - radius, 0), length - win)
                yield q_lo, r, _band_block(ref[0, 0, hh, pl.ds(q_lo, bq), cols],
                                           ref[1, 0, hh, pl.ds(k_lo, win), cols],
                                           ref[2, 0, hh, pl.ds(k_lo, win), cols], bias[kind])

    def strided_phase(hh):
        for p_idx in range(1, len(DILATED_PATTERNS)):
            dil = DILATED_PATTERNS[p_idx][1]
            for q_lo, r, (acc, m, l) in blocks_of(hh, p_idx):
                vals = (acc, jnp.broadcast_to(m, acc.shape), l)
                if dil == d4:
                    rows = pl.ds(q_lo * dil + r, bq, stride=dil)
                    for ref, val in zip(stats, vals):
                        ref[hh, p_idx - 1, rows, :] = val
                else:
                    r4, a4 = r % d4, r // d4
                    for k, val in enumerate(vals):
                        stage_ref[hh, k, r4, pl.ds(a4, bq, stride=d4), :] = val
                    if a4 == d4 - 1:
                        for k, ref in enumerate(stats):
                            ref[hh, p_idx - 1, pl.ds(r4, seq // d4, stride=d4), :] = (
                                stage_ref[hh, k, r4, :, :])
                yield

    def mix_phase(hh):
        out_cols = slice(hh * A_HEAD_DIM, (hh + 1) * A_HEAD_DIM)
        for q_lo, _, (acc0, m0, l0) in blocks_of(hh, 0):
            rs = pl.ds(q_lo, bq)
            m1, m2 = m_ref[hh, 0, rs, :], m_ref[hh, 1, rs, :]
            m = jnp.maximum(jnp.maximum(m0, m1), m2)
            e0, e1, e2 = jnp.exp2(m0 - m), jnp.exp2(m1 - m), jnp.exp2(m2 - m)
            num = e0 * acc0 + e1 * acc_ref[hh, 0, rs, :] + e2 * acc_ref[hh, 1, rs, :]
            den = e0 * l0 + e1 * l_ref[hh, 0, rs, :] + e2 * l_ref[hh, 1, rs, :]
            y_ref[0, rs, out_cols] = (num / den * zg_ref[0, rs, out_cols].astype(F32)).astype(BF16)
            yield

    prev_mix = None
    for hh in range(heads):
        cur = strided_phase(hh)
        if prev_mix is None:
            for _ in cur:
                pass
        else:
            for _ in cur:
                next(prev_mix, None)
                next(cur, None)
            for _ in prev_mix:
                pass
        prev_mix = mix_phase(hh)
    for _ in prev_mix:
        pass


def _dilated(views, zg3, slopes):
    _, bsz, heads, seq, hd = views[0].shape
    assert DILATED_PATTERNS[0][1] == 1 and len(views) == len(DILATED_PATTERNS)
    grp = BAND_HEAD_GROUP
    d4 = DILATED_PATTERNS[1][1]
    assert [p[1] for p in DILATED_PATTERNS] == [1, d4, d4 * d4]
    tables = _band_tables()
    specs = [pl.BlockSpec((3, 1, grp) + v.shape[3:], lambda b, g: (0, b, g, 0, 0)) for v in views]
    tok_spec = pl.BlockSpec((1, seq, grp * hd), lambda b, g: (b, 0, g))
    n_strided = len(DILATED_PATTERNS) - 1
    return pl.pallas_call(
        _dil_kernel,
        grid=(bsz, heads // grp),
        in_specs=[pl.BlockSpec(memory_space=pltpu.SMEM),
                  pl.BlockSpec(tables.shape, lambda b, g: (0, 0, 0))] + specs + [tok_spec],
        out_specs=tok_spec,
        out_shape=jax.ShapeDtypeStruct((bsz, seq, heads * hd), BF16),
        scratch_shapes=([pltpu.VMEM((grp, n_strided, seq, hd), F32) for _ in range(3)]
                        + [pltpu.VMEM((grp, 3, d4, seq // d4, hd), F32)]),
        compiler_params=_params(2),
        name="dilated_mix",
    )(slopes, jnp.asarray(tables), *views, zg3)


def _rope(pe, cc, sa, sb):
    return pe * cc + pltpu.roll(pe, 96, 1) * sa + pltpu.roll(pe, 32, 1) * sb


def _mla_proj_kernel(h_ref, wmid_ref, wbz_ref, wkpe_ref, gq_ref, gkv_ref, wqn_ref, wqp_ref, wkv_ref,
                     cc_ref, sa_ref, sb_ref, zg_ref, bzg_ref, qf_ref, kf_ref, v_ref):
    def proj(w_ref, off, width):
        return _dot_nt(h_ref[...], w_ref[off:off + width, :])

    chunk = PROJ_DOT_COLS
    for c0 in range(0, A_WIDTH, chunk):
        zg_ref[:, c0:c0 + chunk] = _silu(proj(wmid_ref, c0, chunk)).astype(BF16)
    for c0 in range(0, B_WIDTH, chunk):
        bzg_ref[:, c0:c0 + chunk] = _silu(proj(wbz_ref, c0, chunk)).astype(BF16)

    cc, sa, sb = cc_ref[...], sa_ref[...], sb_ref[...]
    q_scale = (B_NOPE_DIM + B_ROPE_DIM) ** -0.5 * LOG2E
    cqn = _rms(proj(wmid_ref, OFF_CQ - OFF_AZ, Q_LORA_RANK), gq_ref[...]).astype(BF16)
    low_half = lax.broadcasted_iota(jnp.int32, (1, LANES), 1) < B_ROPE_DIM
    q_pe = jnp.dot(cqn, wqp_ref[...], preferred_element_type=F32)
    for h0 in range(0, B_HEADS, 2):
        q_nope = jnp.dot(cqn, wqn_ref[:, h0 * B_NOPE_DIM:(h0 + 2) * B_NOPE_DIM],
                         preferred_element_type=F32) * q_scale
        q_rot = _rope(q_pe[:, h0 * B_ROPE_DIM:(h0 + 2) * B_ROPE_DIM], cc, sa, sb) * q_scale
        for h in (h0, h0 + 1):
            own = low_half if h == h0 else jnp.logical_not(low_half)
            qf_ref[0, h, :, :B_NOPE_DIM] = (
                q_nope[:, (h - h0) * B_NOPE_DIM:(h - h0 + 1) * B_NOPE_DIM].astype(BF16))
            qf_ref[0, h, :, B_NOPE_DIM:] = jnp.where(own, q_rot, 0.0).astype(BF16)
    k_rot_low = _rope(proj(wkpe_ref, 0, KPE_PAD), cc, sa, sb)
    k_rot = (k_rot_low.astype(BF16), pltpu.roll(k_rot_low, B_ROPE_DIM, 1).astype(BF16))
    ckvn = _rms(proj(wmid_ref, OFF_CKV - OFF_AZ, KV_LORA_RANK), gkv_ref[...]).astype(BF16)
    kv_w = B_NOPE_DIM + B_V_DIM
    for h in range(B_HEADS):
        kv = jnp.dot(ckvn, wkv_ref[:, h * kv_w:(h + 1) * kv_w], preferred_element_type=F32)
        kf_ref[0, h, :, :B_NOPE_DIM] = kv[:, :B_NOPE_DIM].astype(BF16)
        kf_ref[0, h, :, B_NOPE_DIM:] = k_rot[h % 2]
        v_ref[0, h, :, :] = kv[:, B_NOPE_DIM:].astype(BF16)


def _mla_proj(h, w_mid, w_bz, w_kpe, g_q, g_kv, wq_nope, wq_pe, wkv, cc, sa, sb, bsz, seq):
    n_tok, d = h.shape
    tm = ROW_TILE
    per_b = seq // tm
    row = lambda i: (i, 0)
    full = lambda i: (0, 0)
    pos = lambda i: (i % per_b, 0)
    head_major = lambda i: (i // per_b, 0, i % per_b, 0)
    resident = lambda w: pl.BlockSpec(w.shape, full, pipeline_mode=pl.Buffered(1))
    return pl.pallas_call(
        _mla_proj_kernel,
        grid=(n_tok // tm,),
        in_specs=[
            pl.BlockSpec((tm, d), row),
            resident(w_mid), resident(w_bz), resident(w_kpe),
            pl.BlockSpec((1, Q_LORA_RANK), full),
            pl.BlockSpec((1, KV_LORA_RANK), full),
            resident(wq_nope), resident(wq_pe), resident(wkv),
            pl.BlockSpec((tm, LANES), pos),
            pl.BlockSpec((tm, LANES), pos),
            pl.BlockSpec((tm, LANES), pos),
        ],
        out_specs=[
            pl.BlockSpec((tm, A_WIDTH), row),
            pl.BlockSpec((tm, B_WIDTH), row),
            pl.BlockSpec((1, B_HEADS, tm, B_QK_PAD), head_major),
            pl.BlockSpec((1, B_HEADS, tm, B_QK_PAD), head_major),
            pl.BlockSpec((1, B_HEADS, tm, B_V_DIM), head_major),
        ],
        out_shape=[
            jax.ShapeDtypeStruct((n_tok, A_WIDTH), BF16),
            jax.ShapeDtypeStruct((n_tok, B_WIDTH), BF16),
            jax.ShapeDtypeStruct((bsz, B_HEADS, seq, B_QK_PAD), BF16),
            jax.ShapeDtypeStruct((bsz, B_HEADS, seq, B_QK_PAD), BF16),
            jax.ShapeDtypeStruct((bsz, B_HEADS, seq, B_V_DIM), BF16),
        ],
        compiler_params=_params(1),
        name="mla_proj",
    )(h, w_mid, w_bz, w_kpe, g_q.reshape(1, -1), g_kv.reshape(1, -1), wq_nope, wq_pe, wkv,
      cc, sa, sb)


def _mla_attn_kernel(q_ref, k_ref, v_ref, zg_ref, y_ref, vext_ref, s_ref):
    heads = q_ref.shape[1]
    seq = k_ref.shape[2]
    n_blk = seq // MLA_Q
    for hh in range(heads):
        vext_ref[hh, :, :B_V_DIM] = v_ref[0, hh, :, :]
        vext_ref[hh, :, B_V_DIM:] = jnp.ones((seq, B_V_DIM), BF16)

    def scores(hh, blk, slot):
        s_ref[slot] = _dot_nt(q_ref[0, hh, pl.ds(blk * MLA_Q, MLA_Q), :], k_ref[0, hh, :, :])

    def softmax_pv(hh, blk, slot):
        s = s_ref[slot]
        m = jnp.max(s, axis=-1, keepdims=True)
        p = jnp.exp2(s - m).astype(BF16)
        ol = jnp.dot(p, vext_ref[hh], preferred_element_type=F32)
        o = ol[:, :B_V_DIM] / ol[:, B_V_DIM:]
        rows = pl.ds(blk * MLA_Q, MLA_Q)
        cols = slice(hh * B_V_DIM, (hh + 1) * B_V_DIM)
        y_ref[0, rows, cols] = (o * zg_ref[0, rows, cols].astype(F32)).astype(BF16)

    items = [(hh, blk) for hh in range(heads) for blk in range(n_blk)]
    scores(*items[0], 0)
    for n, item in enumerate(items):
        if n + 1 < len(items):
            scores(*items[n + 1], (n + 1) % 2)
        softmax_pv(*item, n % 2)


def _mla_attn(qf, kf, v, bzg3):
    bsz, heads, seq, _ = qf.shape
    grp = MLA_HEAD_GROUP
    head_blk = lambda b, g: (b, g, 0, 0)
    tok_spec = pl.BlockSpec((1, seq, grp * B_V_DIM), lambda b, g: (b, 0, g))
    return pl.pallas_call(
        _mla_attn_kernel,
        grid=(bsz, heads // grp),
        in_specs=[
            pl.BlockSpec((1, grp, seq, B_QK_PAD), head_blk),
            pl.BlockSpec((1, grp, seq, B_QK_PAD), head_blk),
            pl.BlockSpec((1, grp, seq, B_V_DIM), head_blk),
            tok_spec,
        ],
        out_specs=tok_spec,
        out_shape=jax.ShapeDtypeStruct((bsz, seq, heads * B_V_DIM), BF16),
        scratch_shapes=[pltpu.VMEM((grp, seq, 2 * B_V_DIM), BF16), pltpu.VMEM((2, MLA_Q, seq), F32)],
        compiler_params=_params(2),
        name="mla_attn",
    )(qf, kf, v, bzg3)


def _out_kernel(ya_ref, yb_ref, w_ref, x_ref, mod_ref, g_ref, o_ref):
    tm, d = o_ref.shape
    chunk = PROJ_DOT_COLS
    ssq = jnp.zeros((tm, LANES), F32)
    for c0 in range(0, d, chunk):
        y = (jnp.dot(ya_ref[...], w_ref[:A_WIDTH, c0:c0 + chunk], preferred_element_type=F32)
             + jnp.dot(yb_ref[...], w_ref[A_WIDTH:, c0:c0 + chunk], preferred_element_type=F32))
        o_ref[:, c0:c0 + chunk] = y
        for t0 in range(0, chunk, LANES):
            ssq = ssq + y[:, t0:t0 + LANES] * y[:, t0:t0 + LANES]
    inv = lax.rsqrt(jnp.sum(ssq, axis=-1, keepdims=True) * (1.0 / d) + NORM_EPS)
    gain = mod_ref[0, 2:3, :] * g_ref[...]
    o_ref[...] = x_ref[...] + o_ref[...] * inv * gain


def _out_proj(ya, yb, w_out, x2, mod3, g_post, seq):
    n_tok, d = x2.shape
    tm = ROW_TILE
    per_b = seq // tm
    row = lambda i: (i, 0)
    return pl.pallas_call(
        _out_kernel,
        grid=(n_tok // tm,),
        in_specs=[
            pl.BlockSpec((tm, A_WIDTH), row),
            pl.BlockSpec((tm, B_WIDTH), row),
            pl.BlockSpec(w_out.shape, lambda i: (0, 0), pipeline_mode=pl.Buffered(1)),
            pl.BlockSpec((tm, d), row),
            pl.BlockSpec((1, 3, d), lambda i: (i // per_b, 0, 0)),
            pl.BlockSpec((1, d), lambda i: (0, 0)),
        ],
        out_specs=pl.BlockSpec((tm, d), row),
        out_shape=jax.ShapeDtypeStruct((n_tok, d), F32),
        compiler_params=_params(1),
        name="out_proj",
    )(ya, yb, w_out, x2, mod3, g_post.reshape(1, d))


def _split_w_uq(w_uq):
    r = w_uq.shape[0]
    w = w_uq.reshape(r, B_HEADS, B_NOPE_DIM + B_ROPE_DIM).astype(BF16)
    return (w[:, :, :B_NOPE_DIM].reshape(r, B_HEADS * B_NOPE_DIM),
            w[:, :, B_NOPE_DIM:].reshape(r, B_HEADS * B_ROPE_DIM))


def _rope_tables(seq):
    half = B_ROPE_DIM // 2
    inv_freq = np.power(ROPE_THETA, -np.arange(half, dtype=np.float64) / half)
    ang = np.arange(seq, dtype=np.float64)[:, None] * inv_freq[None, :]
    cos, sin = np.cos(ang), np.sin(ang)
    z = np.zeros_like(cos)
    cc = np.concatenate([cos, cos, cos, cos], axis=1)
    sa = np.concatenate([-sin, z, -sin, z], axis=1)
    sb = np.concatenate([z, sin, z, sin], axis=1)
    return tuple(jnp.asarray(t.astype(np.float32)) for t in (cc, sa, sb))


def _layer(x, c, w_ada, b_ada, g_pre, w_in, g_q_lora, w_uq, g_kv_lora, w_ukv, w_out, g_post):
    bsz, seq, d = x.shape
    x2 = x.reshape(bsz * seq, d)
    mod3 = _ada(c, w_ada, b_ada).reshape(bsz, 3, d)
    w_qkv, w_mid, w_bz, w_kpe = _cast_w_in(w_in.T)
    h, s1, s4, s16 = _qkv_proj(x2, mod3, g_pre, w_qkv, bsz, seq)
    cc, sa, sb = _rope_tables(seq)
    zg, bzg, qf, kf, v = _mla_proj(h, w_mid, w_bz, w_kpe, g_q_lora, g_kv_lora, *_split_w_uq(w_uq),
                                   w_ukv.astype(BF16), cc, sa, sb, bsz, seq)
    slopes = jnp.asarray(np.exp2(-8.0 * np.arange(1, A_HEADS + 1) / A_HEADS).astype(np.float32))
    ya = _dilated((s1, s4, s16), zg.reshape(bsz, seq, A_WIDTH), slopes)
    yb = _mla_attn(qf, kf, v, bzg.reshape(bsz, seq, B_WIDTH))
    out = _out_proj(ya.reshape(bsz * seq, A_WIDTH), yb.reshape(bsz * seq, B_WIDTH),
                    w_out.astype(BF16), x2, mod3, g_post, seq)
    return out.reshape(bsz, seq, d)


def kernel(x, c, w_ada, b_ada, g_pre, w_in, g_q_lora, w_uq, g_kv_lora, w_ukv, w_out, g_post):
    for layer in range(w_ada.shape[0]):
        x = _layer(x, c, w_ada[layer], b_ada[layer], g_pre[layer], w_in[layer], g_q_lora[layer],
                   w_uq[layer], g_kv_lora[layer], w_ukv[layer], w_out[layer], g_post[layer])
    return x
```

```python
import jax
import jax.numpy as jnp
import numpy as np
from jax import lax
from jax.experimental import pallas as pl
from jax.experimental.pallas import tpu as pltpu

F32 = jnp.float32
BF16 = jnp.bfloat16

D_MODEL = 2048
A_HEADS = 8
A_HEAD_DIM = 128
A_WIDTH = A_HEADS * A_HEAD_DIM
DILATED_PATTERNS = ((128, 1), (512, 4), (2048, 16))
B_HEADS = 8
B_V_DIM = 128
B_WIDTH = B_HEADS * B_V_DIM
B_NOPE_DIM = 128
B_ROPE_DIM = 64
Q_LORA_RANK = 512
KV_LORA_RANK = 256
ROPE_THETA = 10000.0
NORM_EPS = 1e-6
NEG_INF = -1e30
LOG2E = 1.4426950408889634

LANES = 128
B_QK_PAD = 256
KPE_PAD = LANES

OFF_AZ = 3 * A_WIDTH
OFF_CQ = OFF_AZ + A_WIDTH
OFF_CKV = OFF_CQ + Q_LORA_RANK

ADA_COL_TILE = 1024
CAST_COL_TILE = 512
ROW_TILE = 512
QKV_HEADS_PER_DOT = 8
PROJ_DOT_COLS = 1024
BAND_Q = 128
BAND_HEAD_GROUP = 2
MLA_Q = 256
MLA_HEAD_GROUP = 2
VMEM_LIMIT = 56 * 1024 * 1024


def _params(grid_rank):
    return pltpu.CompilerParams(dimension_semantics=("arbitrary",) * grid_rank,
                                vmem_limit_bytes=VMEM_LIMIT)


def _silu(v):
    return v * (1.0 / (1.0 + jnp.exp(-v)))


def _rms(v, g):
    return v * lax.rsqrt(jnp.mean(v * v, axis=-1, keepdims=True) + NORM_EPS) * g


def _dot_nt(a, b):
    return lax.dot_general(a, b, (((1,), (1,)), ((), ())), preferred_element_type=F32)


def _ada_kernel(c_ref, w_ref, b_ref, o_ref):
    s = _silu(c_ref[...]).astype(BF16)
    o_ref[...] = jnp.dot(s, w_ref[...].astype(BF16), preferred_element_type=F32) + b_ref[...]


def _ada(c, w_ada, b_ada):
    bsz, d = c.shape
    n = w_ada.shape[1]
    tn = ADA_COL_TILE
    return pl.pallas_call(
        _ada_kernel,
        grid=(n // tn,),
        in_specs=[
            pl.BlockSpec((bsz, d), lambda j: (0, 0)),
            pl.BlockSpec((d, tn), lambda j: (0, j)),
            pl.BlockSpec((1, tn), lambda j: (0, j)),
        ],
        out_specs=pl.BlockSpec((bsz, tn), lambda j: (0, j)),
        out_shape=jax.ShapeDtypeStruct((bsz, n), F32),
        compiler_params=_params(1),
        name="ada_mod",
    )(c, w_ada, b_ada.reshape(1, n))


def _cast_w_in_kernel(w_ref, qkv_ref, mid_ref, bz_ref, kpe_ref):
    kpe0 = OFF_CKV + KV_LORA_RANK
    qkv_ref[...] = w_ref[:OFF_AZ, :].astype(BF16)
    mid_ref[...] = w_ref[OFF_AZ:kpe0, :].astype(BF16)
    bz_ref[...] = w_ref[kpe0 + B_ROPE_DIM:, :].astype(BF16)
    kpe_ref[:B_ROPE_DIM, :] = w_ref[kpe0:kpe0 + B_ROPE_DIM, :].astype(BF16)
    kpe_ref[B_ROPE_DIM:, :] = jnp.zeros((KPE_PAD - B_ROPE_DIM, w_ref.shape[1]), BF16)


def _cast_w_in(w_in_t):
    n, d = w_in_t.shape
    tc = CAST_COL_TILE
    heights = (OFF_AZ, OFF_CKV + KV_LORA_RANK - OFF_AZ, B_WIDTH, KPE_PAD)
    return pl.pallas_call(
        _cast_w_in_kernel,
        grid=(d // tc,),
        in_specs=[pl.BlockSpec((n, tc), lambda j: (0, j))],
        out_specs=[pl.BlockSpec((hgt, tc), lambda j: (0, j)) for hgt in heights],
        out_shape=[jax.ShapeDtypeStruct((hgt, d), BF16) for hgt in heights],
        compiler_params=_params(1),
        name="cast_w_in",
    )(w_in_t)


def _qkv_kernel(x_ref, mod_ref, g_ref, w_ref, h_ref, s1_ref, s4_ref, s16_ref, acc_ref, acc4_ref):
    tm = x_ref.shape[0]
    d4 = DILATED_PATTERNS[1][1]
    x = x_ref[...]
    gain = g_ref[...] * (1.0 + mod_ref[0, 1:2, :])
    inv = lax.rsqrt(jnp.mean(x * x, axis=-1, keepdims=True) + NORM_EPS)
    h_ref[...] = (x * inv * gain + mod_ref[0, 0:1, :]).astype(BF16)

    heads_per = acc_ref.shape[0]
    chunk = heads_per * A_HEAD_DIM
    q_scale = A_HEAD_DIM ** -0.5 * LOG2E
    for t in range(3):
        for c0 in range(0, A_WIDTH, chunk):
            acc = _dot_nt(h_ref[...], w_ref[t * A_WIDTH + c0:t * A_WIDTH + c0 + chunk, :])
            if t == 0:
                acc = acc * q_scale
            for k in range(heads_per):
                head = c0 // A_HEAD_DIM + k
                a = acc[:, k * A_HEAD_DIM:(k + 1) * A_HEAD_DIM]
                s1_ref[t, 0, head, :, :] = a.astype(BF16)
                acc_ref[k, :, :] = a
            for k in range(heads_per):
                head = c0 // A_HEAD_DIM + k
                for r4 in range(d4):
                    g4 = acc_ref[k, pl.ds(r4, tm // d4, stride=d4), :]
                    s4_ref[t, 0, head, :, r4 * LANES:(r4 + 1) * LANES] = g4.astype(BF16)
                    acc4_ref[k, r4, :, :] = g4
                for r4 in range(d4):
                    for a4 in range(d4):
                        r = r4 + d4 * a4
                        s16_ref[t, 0, head, :, r * LANES:(r + 1) * LANES] = (
                            acc4_ref[k, r4, pl.ds(a4, tm // (d4 * d4), stride=d4), :].astype(BF16))


def _qkv_proj(x2, mod3, g_pre, w_qkv, bsz, seq):
    n_tok, d = x2.shape
    tm = ROW_TILE
    per_b = seq // tm
    hd = A_HEAD_DIM
    d4, d16 = DILATED_PATTERNS[1][1], DILATED_PATTERNS[2][1]
    assert d16 == d4 * d4
    head_major = lambda i: (0, i // per_b, 0, i % per_b, 0)
    return pl.pallas_call(
        _qkv_kernel,
        grid=(n_tok // tm,),
        in_specs=[
            pl.BlockSpec((tm, d), lambda i: (i, 0)),
            pl.BlockSpec((1, 3, d), lambda i: (i // per_b, 0, 0)),
            pl.BlockSpec((1, d), lambda i: (0, 0)),
            pl.BlockSpec(w_qkv.shape, lambda i: (0, 0), pipeline_mode=pl.Buffered(1)),
        ],
        out_specs=[
            pl.BlockSpec((tm, d), lambda i: (i, 0)),
            pl.BlockSpec((3, 1, A_HEADS, tm, hd), head_major),
            pl.BlockSpec((3, 1, A_HEADS, tm // d4, d4 * hd), head_major),
            pl.BlockSpec((3, 1, A_HEADS, tm // d16, d16 * hd), head_major),
        ],
        out_shape=[
            jax.ShapeDtypeStruct((n_tok, d), BF16),
            jax.ShapeDtypeStruct((3, bsz, A_HEADS, seq, hd), BF16),
            jax.ShapeDtypeStruct((3, bsz, A_HEADS, seq // d4, d4 * hd), BF16),
            jax.ShapeDtypeStruct((3, bsz, A_HEADS, seq // d16, d16 * hd), BF16),
        ],
        scratch_shapes=[pltpu.VMEM((QKV_HEADS_PER_DOT, tm, hd), F32),
                        pltpu.VMEM((QKV_HEADS_PER_DOT, d4, tm // d4, hd), F32)],
        compiler_params=_params(1),
        name="qkv_proj",
    )(x2, mod3, g_pre.reshape(1, d), w_qkv)


def _band_tables():
    bq, win = BAND_Q, 2 * BAND_Q
    radius = DILATED_PATTERNS[0][0] // 2 // DILATED_PATTERNS[0][1]
    rows = np.arange(bq)[:, None]
    cols = np.arange(win)[None, :]
    tabs = []
    for k_minus_q in (0, -radius, -bq, 0):
        dist = np.abs(cols + k_minus_q - rows)
        tabs.append(np.where(dist <= radius, -LOG2E * dist, NEG_INF))
    tabs[3][:, bq:] = NEG_INF
    return np.stack(tabs).astype(np.float32)


def _band_block(q, k_win, v_win, bias):
    s = _dot_nt(q, k_win) + bias
    m = jnp.max(s, axis=-1, keepdims=True)
    p = jnp.exp2(s - m).astype(BF16)
    v_ext = jnp.concatenate([v_win, jnp.ones_like(v_win)], axis=1)
    ol = jnp.dot(p, v_ext, preferred_element_type=F32)
    return ol[:, :A_HEAD_DIM], m, ol[:, A_HEAD_DIM:]


def _dil_kernel(slope_ref, tab_ref, s1_ref, s4_ref, s16_ref, zg_ref, y_ref,
                acc_ref, m_ref, l_ref, stage_ref):
    heads = s1_ref.shape[2]
    seq = s1_ref.shape[3]
    bq = BAND_Q
    win = 2 * bq
    views = (s1_ref, s4_ref, s16_ref)
    stats = (acc_ref, m_ref, l_ref)
    d4 = DILATED_PATTERNS[1][1]

    def blocks_of(hh, p_idx):
        window, dil = DILATED_PATTERNS[p_idx]
        radius = window // 2 // dil
        length = seq // dil
        ref = views[p_idx]
        pen = slope_ref[pl.program_id(1) * heads + hh] * float(dil)
        n_blk = length // bq
        def scaled(tab):
            return jnp.where(tab > 0.5 * NEG_INF, pen * tab, NEG_INF)

        bias = ([scaled(tab_ref[3, :, :bq])] if n_blk == 1 else
                [scaled(tab_ref[kind]) for kind in range(3)])
        for r in sorted(range(dil), key=lambda r: (r % d4, r)):
            cols = slice(r * LANES, (r + 1) * LANES)
            if n_blk == 1:
                yield 0, r, _band_block(ref[0, 0, hh, :, cols], ref[1, 0, hh, :, cols],
                                        ref[2, 0, hh, :, cols], bias[0])
                continue
            for qi in range(n_blk):
                kind = 0 if qi == 0 else (2 if qi == n_blk - 1 else 1)
                q_lo = qi * bq
                k_lo = min(max(q_lo - radius, 0), length - win)
                yield q_lo, r, _band_block(ref[0, 0, hh, pl.ds(q_lo, bq), cols],
                                           ref[1, 0, hh, pl.ds(k_lo, win), cols],
                                           ref[2, 0, hh, pl.ds(k_lo, win), cols], bias[kind])

    def strided_phase(hh):
        for p_idx in range(1, len(DILATED_PATTERNS)):
            dil = DILATED_PATTERNS[p_idx][1]
            for q_lo, r, (acc, m, l) in blocks_of(hh, p_idx):
                vals = (acc, jnp.broadcast_to(m, acc.shape), l)
                if dil == d4:
                    rows = pl.ds(q_lo * dil + r, bq, stride=dil)
                    for ref, val in zip(stats, vals):
                        ref[hh, p_idx - 1, rows, :] = val
                else:
                    r4, a4 = r % d4, r // d4
                    for k, val in enumerate(vals):
                        stage_ref[hh, k, r4, pl.ds(a4, bq, stride=d4), :] = val
                    if a4 == d4 - 1:
                        for k, ref in enumerate(stats):
                            ref[hh, p_idx - 1, pl.ds(r4, seq // d4, stride=d4), :] = (
                                stage_ref[hh, k, r4, :, :])
                yield

    def mix_phase(hh):
        out_cols = slice(hh * A_HEAD_DIM, (hh + 1) * A_HEAD_DIM)
        for q_lo, _, (acc0, m0, l0) in blocks_of(hh, 0):
            rs = pl.ds(q_lo, bq)
            m1, m2 = m_ref[hh, 0, rs, :], m_ref[hh, 1, rs, :]
            m = jnp.maximum(jnp.maximum(m0, m1), m2)
            e0, e1, e2 = jnp.exp2(m0 - m), jnp.exp2(m1 - m), jnp.exp2(m2 - m)
            num = e0 * acc0 + e1 * acc_ref[hh, 0, rs, :] + e2 * acc_ref[hh, 1, rs, :]
            den = e0 * l0 + e1 * l_ref[hh, 0, rs, :] + e2 * l_ref[hh, 1, rs, :]
            y_ref[0, rs, out_cols] = (num / den * zg_ref[0, rs, out_cols].astype(F32)).astype(BF16)
            yield

    prev_mix = None
    for hh in range(heads):
        cur = strided_phase(hh)
        if prev_mix is None:
            for _ in cur:
                pass
        else:
            for _ in cur:
                next(prev_mix, None)
                next(cur, None)
            for _ in prev_mix:
                pass
        prev_mix = mix_phase(hh)
    for _ in prev_mix:
        pass


def _dilated(views, zg3, slopes):
    _, bsz, heads, seq, hd = views[0].shape
    assert DILATED_PATTERNS[0][1] == 1 and len(views) == len(DILATED_PATTERNS)
    grp = BAND_HEAD_GROUP
    d4 = DILATED_PATTERNS[1][1]
    assert [p[1] for p in DILATED_PATTERNS] == [1, d4, d4 * d4]
    tables = _band_tables()
    specs = [pl.BlockSpec((3, 1, grp) + v.shape[3:], lambda b, g: (0, b, g, 0, 0)) for v in views]
    tok_spec = pl.BlockSpec((1, seq, grp * hd), lambda b, g: (b, 0, g))
    n_strided = len(DILATED_PATTERNS) - 1
    return pl.pallas_call(
        _dil_kernel,
        grid=(bsz, heads // grp),
        in_specs=[pl.BlockSpec(memory_space=pltpu.SMEM),
                  pl.BlockSpec(tables.shape, lambda b, g: (0, 0, 0))] + specs + [tok_spec],
        out_specs=tok_spec,
        out_shape=jax.ShapeDtypeStruct((bsz, seq, heads * hd), BF16),
        scratch_shapes=([pltpu.VMEM((grp, n_strided, seq, hd), F32) for _ in range(3)]
                        + [pltpu.VMEM((grp, 3, d4, seq // d4, hd), F32)]),
        compiler_params=_params(2),
        name="dilated_mix",
    )(slopes, jnp.asarray(tables), *views, zg3)


def _rope(pe, cc, sa, sb):
    return pe * cc + pltpu.roll(pe, 96, 1) * sa + pltpu.roll(pe, 32, 1) * sb


def _mla_proj_kernel(h_ref, wmid_ref, wbz_ref, wkpe_ref, gq_ref, gkv_ref, wqn_ref, wqp_ref, wkv_ref,
                     cc_ref, sa_ref, sb_ref, zg_ref, bzg_ref, qf_ref, kf_ref, v_ref):
    def proj(w_ref, off, width):
        return _dot_nt(h_ref[...], w_ref[off:off + width, :])

    chunk = PROJ_DOT_COLS
    cc, sa, sb = cc_ref[...], sa_ref[...], sb_ref[...]
    q_scale = (B_NOPE_DIM + B_ROPE_DIM) ** -0.5 * LOG2E
    cqn = _rms(proj(wmid_ref, OFF_CQ - OFF_AZ, Q_LORA_RANK), gq_ref[...]).astype(BF16)
    ckvn = _rms(proj(wmid_ref, OFF_CKV - OFF_AZ, KV_LORA_RANK), gkv_ref[...]).astype(BF16)
    k_rot_low = _rope(proj(wkpe_ref, 0, KPE_PAD), cc, sa, sb)
    for c0 in range(0, A_WIDTH, chunk):
        zg_ref[:, c0:c0 + chunk] = _silu(proj(wmid_ref, c0, chunk)).astype(BF16)
    for c0 in range(0, B_WIDTH, chunk):
        bzg_ref[:, c0:c0 + chunk] = _silu(proj(wbz_ref, c0, chunk)).astype(BF16)
    low_half = lax.broadcasted_iota(jnp.int32, (1, LANES), 1) < B_ROPE_DIM
    q_pe = jnp.dot(cqn, wqp_ref[...], preferred_element_type=F32)
    for h0 in range(0, B_HEADS, 2):
        q_nope = jnp.dot(cqn, wqn_ref[:, h0 * B_NOPE_DIM:(h0 + 2) * B_NOPE_DIM],
                         preferred_element_type=F32) * q_scale
        q_rot = _rope(q_pe[:, h0 * B_ROPE_DIM:(h0 + 2) * B_ROPE_DIM], cc, sa, sb) * q_scale
        for h in (h0, h0 + 1):
            own = low_half if h == h0 else jnp.logical_not(low_half)
            qf_ref[0, h, :, :B_NOPE_DIM] = (
                q_nope[:, (h - h0) * B_NOPE_DIM:(h - h0 + 1) * B_NOPE_DIM].astype(BF16))
            qf_ref[0, h, :, B_NOPE_DIM:] = jnp.where(own, q_rot, 0.0).astype(BF16)
    k_rot = (k_rot_low.astype(BF16), pltpu.roll(k_rot_low, B_ROPE_DIM, 1).astype(BF16))
    kv_w = B_NOPE_DIM + B_V_DIM
    for h in range(B_HEADS):
        kv = jnp.dot(ckvn, wkv_ref[:, h * kv_w:(h + 1) * kv_w], preferred_element_type=F32)
        kf_ref[0, h, :, :B_NOPE_DIM] = kv[:, :B_NOPE_DIM].astype(BF16)
        kf_ref[0, h, :, B_NOPE_DIM:] = k_rot[h % 2]
        v_ref[0, h, :, :] = kv[:, B_NOPE_DIM:].astype(BF16)


def _mla_proj(h, w_mid, w_bz, w_kpe, g_q, g_kv, wq_nope, wq_pe, wkv, cc, sa, sb, bsz, seq):
    n_tok, d = h.shape
    tm = ROW_TILE
    per_b = seq // tm
    row = lambda i: (i, 0)
    full = lambda i: (0, 0)
    pos = lambda i: (i % per_b, 0)
    head_major = lambda i: (i // per_b, 0, i % per_b, 0)
    resident = lambda w: pl.BlockSpec(w.shape, full, pipeline_mode=pl.Buffered(1))
    return pl.pallas_call(
        _mla_proj_kernel,
        grid=(n_tok // tm,),
        in_specs=[
            pl.BlockSpec((tm, d), row),
            resident(w_mid), resident(w_bz), resident(w_kpe),
            pl.BlockSpec((1, Q_LORA_RANK), full),
            pl.BlockSpec((1, KV_LORA_RANK), full),
            resident(wq_nope), resident(wq_pe), resident(wkv),
            pl.BlockSpec((tm, LANES), pos),
            pl.BlockSpec((tm, LANES), pos),
            pl.BlockSpec((tm, LANES), pos),
        ],
        out_specs=[
            pl.BlockSpec((tm, A_WIDTH), row),
            pl.BlockSpec((tm, B_WIDTH), row),
            pl.BlockSpec((1, B_HEADS, tm, B_QK_PAD), head_major),
            pl.BlockSpec((1, B_HEADS, tm, B_QK_PAD), head_major),
            pl.BlockSpec((1, B_HEADS, tm, B_V_DIM), head_major),
        ],
        out_shape=[
            jax.ShapeDtypeStruct((n_tok, A_WIDTH), BF16),
            jax.ShapeDtypeStruct((n_tok, B_WIDTH), BF16),
            jax.ShapeDtypeStruct((bsz, B_HEADS, seq, B_QK_PAD), BF16),
            jax.ShapeDtypeStruct((bsz, B_HEADS, seq, B_QK_PAD), BF16),
            jax.ShapeDtypeStruct((bsz, B_HEADS, seq, B_V_DIM), BF16),
        ],
        compiler_params=_params(1),
        name="mla_proj",
    )(h, w_mid, w_bz, w_kpe, g_q.reshape(1, -1), g_kv.reshape(1, -1), wq_nope, wq_pe, wkv,
      cc, sa, sb)


def _mla_attn_kernel(q_ref, k_ref, v_ref, zg_ref, y_ref, vext_ref, s_ref):
    heads = q_ref.shape[1]
    seq = k_ref.shape[2]
    n_blk = seq // MLA_Q
    for hh in range(heads):
        vext_ref[hh, :, :B_V_DIM] = v_ref[0, hh, :, :]
        vext_ref[hh, :, B_V_DIM:] = jnp.ones((seq, B_V_DIM), BF16)

    def scores(hh, blk, slot):
        s_ref[slot] = _dot_nt(q_ref[0, hh, pl.ds(blk * MLA_Q, MLA_Q), :], k_ref[0, hh, :, :])

    def softmax_pv(hh, blk, slot):
        s = s_ref[slot]
        m = jnp.max(s, axis=-1, keepdims=True)
        p = jnp.exp2(s - m).astype(BF16)
        ol = jnp.dot(p, vext_ref[hh], preferred_element_type=F32)
        o = ol[:, :B_V_DIM] / ol[:, B_V_DIM:]
        rows = pl.ds(blk * MLA_Q, MLA_Q)
        cols = slice(hh * B_V_DIM, (hh + 1) * B_V_DIM)
        y_ref[0, rows, cols] = (o * zg_ref[0, rows, cols].astype(F32)).astype(BF16)

    items = [(hh, blk) for hh in range(heads) for blk in range(n_blk)]
    scores(*items[0], 0)
    for n, item in enumerate(items):
        if n + 1 < len(items):
            scores(*items[n + 1], (n + 1) % 2)
        softmax_pv(*item, n % 2)


def _mla_attn(qf, kf, v, bzg3):
    bsz, heads, seq, _ = qf.shape
    grp = MLA_HEAD_GROUP
    head_blk = lambda b, g: (b, g, 0, 0)
    tok_spec = pl.BlockSpec((1, seq, grp * B_V_DIM), lambda b, g: (b, 0, g))
    return pl.pallas_call(
        _mla_attn_kernel,
        grid=(bsz, heads // grp),
        in_specs=[
            pl.BlockSpec((1, grp, seq, B_QK_PAD), head_blk),
            pl.BlockSpec((1, grp, seq, B_QK_PAD), head_blk),
            pl.BlockSpec((1, grp, seq, B_V_DIM), head_blk),
            tok_spec,
        ],
        out_specs=tok_spec,
        out_shape=jax.ShapeDtypeStruct((bsz, seq, heads * B_V_DIM), BF16),
        scratch_shapes=[pltpu.VMEM((grp, seq, 2 * B_V_DIM), BF16), pltpu.VMEM((2, MLA_Q, seq), F32)],
        compiler_params=_params(2),
        name="mla_attn",
    )(qf, kf, v, bzg3)


def _out_kernel(ya_ref, yb_ref, w_ref, x_ref, mod_ref, g_ref, o_ref, y_ref):
    tm, d = y_ref.shape
    chunk = PROJ_DOT_COLS
    ssq = jnp.zeros((tm, LANES), F32)
    for c0 in range(0, d, chunk):
        y = (jnp.dot(ya_ref[...], w_ref[:A_WIDTH, c0:c0 + chunk], preferred_element_type=F32)
             + jnp.dot(yb_ref[...], w_ref[A_WIDTH:, c0:c0 + chunk], preferred_element_type=F32))
        y_ref[:, c0:c0 + chunk] = y
        for t0 in range(0, chunk, LANES):
            ssq = ssq + y[:, t0:t0 + LANES] * y[:, t0:t0 + LANES]
    inv = lax.rsqrt(jnp.sum(ssq, axis=-1, keepdims=True) * (1.0 / d) + NORM_EPS)
    gain = mod_ref[0, 2:3, :] * g_ref[...]
    o_ref[...] = x_ref[...] + y_ref[...] * inv * gain


def _out_proj(ya, yb, w_out, x2, mod3, g_post, seq):
    n_tok, d = x2.shape
    tm = ROW_TILE
    per_b = seq // tm
    row = lambda i: (i, 0)
    return pl.pallas_call(
        _out_kernel,
        grid=(n_tok // tm,),
        in_specs=[
            pl.BlockSpec((tm, A_WIDTH), row),
            pl.BlockSpec((tm, B_WIDTH), row),
            pl.BlockSpec(w_out.shape, lambda i: (0, 0), pipeline_mode=pl.Buffered(1)),
            pl.BlockSpec((tm, d), row),
            pl.BlockSpec((1, 3, d), lambda i: (i // per_b, 0, 0)),
            pl.BlockSpec((1, d), lambda i: (0, 0)),
        ],
        out_specs=pl.BlockSpec((tm, d), row),
        out_shape=jax.ShapeDtypeStruct((n_tok, d), F32),
        scratch_shapes=[pltpu.VMEM((tm, d), F32)],
        compiler_params=_params(1),
        name="out_proj",
    )(ya, yb, w_out, x2, mod3, g_post.reshape(1, d))


def _split_w_uq(w_uq):
    r = w_uq.shape[0]
    w = w_uq.reshape(r, B_HEADS, B_NOPE_DIM + B_ROPE_DIM).astype(BF16)
    return (w[:, :, :B_NOPE_DIM].reshape(r, B_HEADS * B_NOPE_DIM),
            w[:, :, B_NOPE_DIM:].reshape(r, B_HEADS * B_ROPE_DIM))


def _rope_tables(seq):
    half = B_ROPE_DIM // 2
    inv_freq = np.power(ROPE_THETA, -np.arange(half, dtype=np.float64) / half)
    ang = np.arange(seq, dtype=np.float64)[:, None] * inv_freq[None, :]
    cos, sin = np.cos(ang), np.sin(ang)
    z = np.zeros_like(cos)
    cc = np.concatenate([cos, cos, cos, cos], axis=1)
    sa = np.concatenate([-sin, z, -sin, z], axis=1)
    sb = np.concatenate([z, sin, z, sin], axis=1)
    return tuple(jnp.asarray(t.astype(np.float32)) for t in (cc, sa, sb))


def _layer(x, c, w_ada, b_ada, g_pre, w_in, g_q_lora, w_uq, g_kv_lora, w_ukv, w_out, g_post):
    bsz, seq, d = x.shape
    x2 = x.reshape(bsz * seq, d)
    mod3 = _ada(c, w_ada, b_ada).reshape(bsz, 3, d)
    w_qkv, w_mid, w_bz, w_kpe = _cast_w_in(w_in.T)
    h, s1, s4, s16 = _qkv_proj(x2, mod3, g_pre, w_qkv, bsz, seq)
    cc, sa, sb = _rope_tables(seq)
    zg, bzg, qf, kf, v = _mla_proj(h, w_mid, w_bz, w_kpe, g_q_lora, g_kv_lora, *_split_w_uq(w_uq),
                                   w_ukv.astype(BF16), cc, sa, sb, bsz, seq)
    slopes = jnp.asarray(np.exp2(-8.0 * np.arange(1, A_HEADS + 1) / A_HEADS).astype(np.float32))
    ya = _dilated((s1, s4, s16), zg.reshape(bsz, seq, A_WIDTH), slopes)
    yb = _mla_attn(qf, kf, v, bzg.reshape(bsz, seq, B_WIDTH))
    out = _out_proj(ya.reshape(bsz * seq, A_WIDTH), yb.reshape(bsz * seq, B_WIDTH),
                    w_out.astype(BF16), x2, mod3, g_post, seq)
    return out.reshape(bsz, seq, d)


def kernel(x, c, w_ada, b_ada, g_pre, w_in, g_q_lora, w_uq, g_kv_lora, w_ukv, w_out, g_post):
    for layer in range(w_ada.shape[0]):
        x = _layer(x, c, w_ada[layer], b_ada[layer], g_pre[layer], w_in[layer], g_q_lora[layer],
                   w_uq[layer], g_kv_lora[layer], w_ukv[layer], w_out[layer], g_post[layer])
    return x
```

```python
import jax
import jax.numpy as jnp
import numpy as np
from jax import lax
from jax.experimental import pallas as pl
from jax.experimental.pallas import tpu as pltpu

F32 = jnp.float32
BF16 = jnp.bfloat16

D_MODEL = 2048
A_HEADS = 8
A_HEAD_DIM = 128
A_WIDTH = A_HEADS * A_HEAD_DIM
DILATED_PATTERNS = ((128, 1), (512, 4), (2048, 16))
B_HEADS = 8
B_V_DIM = 128
B_WIDTH = B_HEADS * B_V_DIM
B_NOPE_DIM = 128
B_ROPE_DIM = 64
Q_LORA_RANK = 512
KV_LORA_RANK = 256
ROPE_THETA = 10000.0
NORM_EPS = 1e-6
NEG_INF = -1e30
LOG2E = 1.4426950408889634

LANES = 128
B_QK_PAD = 256
KPE_PAD = LANES

OFF_AZ = 3 * A_WIDTH
OFF_CQ = OFF_AZ + A_WIDTH
OFF_CKV = OFF_CQ + Q_LORA_RANK

ADA_COL_TILE = 1024
CAST_COL_TILE = 512
ROW_TILE = 512
QKV_HEADS_PER_DOT = 8
PROJ_DOT_COLS = 1024
BAND_Q = 128
BAND_HEAD_GROUP = 2
MLA_Q = 256
MLA_HEAD_GROUP = 2
VMEM_LIMIT = 56 * 1024 * 1024


def _params(grid_rank):
    return pltpu.CompilerParams(dimension_semantics=("arbitrary",) * grid_rank,
                                vmem_limit_bytes=VMEM_LIMIT)


def _silu(v):
    return v * (1.0 / (1.0 + jnp.exp(-v)))


def _rms(v, g):
    return v * lax.rsqrt(jnp.mean(v * v, axis=-1, keepdims=True) + NORM_EPS) * g


def _dot_nt(a, b):
    return lax.dot_general(a, b, (((1,), (1,)), ((), ())), preferred_element_type=F32)


def _ada_kernel(c_ref, w_ref, b_ref, o_ref):
    s = _silu(c_ref[...]).astype(BF16)
    o_ref[...] = jnp.dot(s, w_ref[...].astype(BF16), preferred_element_type=F32) + b_ref[...]


def _ada(c, w_ada, b_ada):
    bsz, d = c.shape
    n = w_ada.shape[1]
    tn = ADA_COL_TILE
    return pl.pallas_call(
        _ada_kernel,
        grid=(n // tn,),
        in_specs=[
            pl.BlockSpec((bsz, d), lambda j: (0, 0)),
            pl.BlockSpec((d, tn), lambda j: (0, j)),
            pl.BlockSpec((1, tn), lambda j: (0, j)),
        ],
        out_specs=pl.BlockSpec((bsz, tn), lambda j: (0, j)),
        out_shape=jax.ShapeDtypeStruct((bsz, n), F32),
        compiler_params=_params(1),
        name="ada_mod",
    )(c, w_ada, b_ada.reshape(1, n))


def _cast_w_in_kernel(w_ref, qkv_ref, mid_ref, bz_ref, kpe_ref):
    kpe0 = OFF_CKV + KV_LORA_RANK
    qkv_ref[...] = w_ref[:OFF_AZ, :].astype(BF16)
    mid_ref[...] = w_ref[OFF_AZ:kpe0, :].astype(BF16)
    bz_ref[...] = w_ref[kpe0 + B_ROPE_DIM:, :].astype(BF16)
    kpe_ref[:B_ROPE_DIM, :] = w_ref[kpe0:kpe0 + B_ROPE_DIM, :].astype(BF16)
    kpe_ref[B_ROPE_DIM:, :] = jnp.zeros((KPE_PAD - B_ROPE_DIM, w_ref.shape[1]), BF16)


def _cast_w_in(w_in_t):
    n, d = w_in_t.shape
    tc = CAST_COL_TILE
    heights = (OFF_AZ, OFF_CKV + KV_LORA_RANK - OFF_AZ, B_WIDTH, KPE_PAD)
    return pl.pallas_call(
        _cast_w_in_kernel,
        grid=(d // tc,),
        in_specs=[pl.BlockSpec((n, tc), lambda j: (0, j))],
        out_specs=[pl.BlockSpec((hgt, tc), lambda j: (0, j)) for hgt in heights],
        out_shape=[jax.ShapeDtypeStruct((hgt, d), BF16) for hgt in heights],
        compiler_params=_params(1),
        name="cast_w_in",
    )(w_in_t)


def _qkv_kernel(x_ref, mod_ref, g_ref, w_ref, h_ref, s1_ref, s4_ref, s16_ref, acc_ref, acc4_ref):
    tm = x_ref.shape[0]
    d4 = DILATED_PATTERNS[1][1]
    x = x_ref[...]
    gain = g_ref[...] * (1.0 + mod_ref[0, 1:2, :])
    inv = lax.rsqrt(jnp.mean(x * x, axis=-1, keepdims=True) + NORM_EPS)
    h_ref[...] = (x * inv * gain + mod_ref[0, 0:1, :]).astype(BF16)

    heads_per = acc_ref.shape[0]
    chunk = heads_per * A_HEAD_DIM
    q_scale = A_HEAD_DIM ** -0.5 * LOG2E
    for t in range(3):
        for c0 in range(0, A_WIDTH, chunk):
            acc = _dot_nt(h_ref[...], w_ref[t * A_WIDTH + c0:t * A_WIDTH + c0 + chunk, :])
            if t == 0:
                acc = acc * q_scale
            for k in range(heads_per):
                head = c0 // A_HEAD_DIM + k
                a = acc[:, k * A_HEAD_DIM:(k + 1) * A_HEAD_DIM]
                s1_ref[t, 0, head, :, :] = a.astype(BF16)
                acc_ref[k, :, :] = a
            for k in range(heads_per):
                head = c0 // A_HEAD_DIM + k
                for r4 in range(d4):
                    g4 = acc_ref[k, pl.ds(r4, tm // d4, stride=d4), :]
                    s4_ref[t, 0, head, :, r4 * LANES:(r4 + 1) * LANES] = g4.astype(BF16)
                    acc4_ref[k, r4, :, :] = g4
                for r4 in range(d4):
                    for a4 in range(d4):
                        r = r4 + d4 * a4
                        s16_ref[t, 0, head, :, r * LANES:(r + 1) * LANES] = (
                            acc4_ref[k, r4, pl.ds(a4, tm // (d4 * d4), stride=d4), :].astype(BF16))


def _qkv_proj(x2, mod3, g_pre, w_qkv, bsz, seq):
    n_tok, d = x2.shape
    tm = ROW_TILE
    per_b = seq // tm
    hd = A_HEAD_DIM
    d4, d16 = DILATED_PATTERNS[1][1], DILATED_PATTERNS[2][1]
    assert d16 == d4 * d4
    head_major = lambda i: (0, i // per_b, 0, i % per_b, 0)
    return pl.pallas_call(
        _qkv_kernel,
        grid=(n_tok // tm,),
        in_specs=[
            pl.BlockSpec((tm, d), lambda i: (i, 0)),
            pl.BlockSpec((1, 3, d), lambda i: (i // per_b, 0, 0)),
            pl.BlockSpec((1, d), lambda i: (0, 0)),
            pl.BlockSpec(w_qkv.shape, lambda i: (0, 0), pipeline_mode=pl.Buffered(1)),
        ],
        out_specs=[
            pl.BlockSpec((tm, d), lambda i: (i, 0)),
            pl.BlockSpec((3, 1, A_HEADS, tm, hd), head_major),
            pl.BlockSpec((3, 1, A_HEADS, tm // d4, d4 * hd), head_major),
            pl.BlockSpec((3, 1, A_HEADS, tm // d16, d16 * hd), head_major),
        ],
        out_shape=[
            jax.ShapeDtypeStruct((n_tok, d), BF16),
            jax.ShapeDtypeStruct((3, bsz, A_HEADS, seq, hd), BF16),
            jax.ShapeDtypeStruct((3, bsz, A_HEADS, seq // d4, d4 * hd), BF16),
            jax.ShapeDtypeStruct((3, bsz, A_HEADS, seq // d16, d16 * hd), BF16),
        ],
        scratch_shapes=[pltpu.VMEM((QKV_HEADS_PER_DOT, tm, hd), F32),
                        pltpu.VMEM((QKV_HEADS_PER_DOT, d4, tm // d4, hd), F32)],
        compiler_params=_params(1),
        name="qkv_proj",
    )(x2, mod3, g_pre.reshape(1, d), w_qkv)


def _band_tables():
    bq, win = BAND_Q, 2 * BAND_Q
    radius = DILATED_PATTERNS[0][0] // 2 // DILATED_PATTERNS[0][1]
    rows = np.arange(bq)[:, None]
    cols = np.arange(win)[None, :]
    tabs = []
    for k_minus_q in (0, -radius, -bq, 0):
        dist = np.abs(cols + k_minus_q - rows)
        tabs.append(np.where(dist <= radius, -LOG2E * dist, NEG_INF))
    tabs[3][:, bq:] = NEG_INF
    return np.stack(tabs).astype(np.float32)


def _band_block(q, k_win, v_win, bias):
    s = _dot_nt(q, k_win) + bias
    m = jnp.max(s, axis=-1, keepdims=True)
    p = jnp.exp2(s - m).astype(BF16)
    v_ext = jnp.concatenate([v_win, jnp.ones_like(v_win)], axis=1)
    ol = jnp.dot(p, v_ext, preferred_element_type=F32)
    return ol[:, :A_HEAD_DIM], m, ol[:, A_HEAD_DIM:]


def _dil_kernel(slope_ref, tab_ref, s1_ref, s4_ref, s16_ref, zg_ref, y_ref,
                acc_ref, m_ref, l_ref, stage_ref):
    heads = s1_ref.shape[2]
    seq = s1_ref.shape[3]
    bq = BAND_Q
    win = 2 * bq
    views = (s1_ref, s4_ref, s16_ref)
    stats = (acc_ref, m_ref, l_ref)
    d4 = DILATED_PATTERNS[1][1]

    def blocks_of(hh, p_idx):
        window, dil = DILATED_PATTERNS[p_idx]
        radius = window // 2 // dil
        length = seq // dil
        ref = views[p_idx]
        pen = slope_ref[pl.program_id(1) * heads + hh] * float(dil)
        n_blk = length // bq
        def scaled(tab):
            return jnp.where(tab > 0.5 * NEG_INF, pen * tab, NEG_INF)

        bias = ([scaled(tab_ref[3, :, :bq])] if n_blk == 1 else
                [scaled(tab_ref[kind]) for kind in range(3)])
        for r in sorted(range(dil), key=lambda r: (r % d4, r)):
            cols = slice(r * LANES, (r + 1) * LANES)
            if n_blk == 1:
                yield 0, r, _band_block(ref[0, 0, hh, :, cols], ref[1, 0, hh, :, cols],
                                        ref[2, 0, hh, :, cols], bias[0])
                continue
            for qi in range(n_blk):
                kind = 0 if qi == 0 else (2 if qi == n_blk - 1 else 1)
                q_lo = qi * bq
                k_lo = min(max(q_lo - radius, 0), length - win)
                yield q_lo, r, _band_block(ref[0, 0, hh, pl.ds(q_lo, bq), cols],
                                           ref[1, 0, hh, pl.ds(k_lo, win), cols],
                                           ref[2, 0, hh, pl.ds(k_lo, win), cols], bias[kind])

    def strided_phase(hh):
        for p_idx in range(1, len(DILATED_PATTERNS)):
            dil = DILATED_PATTERNS[p_idx][1]
            for q_lo, r, (acc, m, l) in blocks_of(hh, p_idx):
                vals = (acc, jnp.broadcast_to(m, acc.shape), l)
                if dil == d4:
                    rows = pl.ds(q_lo * dil + r, bq, stride=dil)
                    for ref, val in zip(stats, vals):
                        ref[hh, p_idx - 1, rows, :] = val
                else:
                    r4, a4 = r % d4, r // d4
                    for k, val in enumerate(vals):
                        stage_ref[hh, k, r4, pl.ds(a4, bq, stride=d4), :] = val
                    if a4 == d4 - 1:
                        for k, ref in enumerate(stats):
                            ref[hh, p_idx - 1, pl.ds(r4, seq // d4, stride=d4), :] = (
                                stage_ref[hh, k, r4, :, :])
                yield

    def mix_phase(hh):
        out_cols = slice(hh * A_HEAD_DIM, (hh + 1) * A_HEAD_DIM)
        for q_lo, _, (acc0, m0, l0) in blocks_of(hh, 0):
            rs = pl.ds(q_lo, bq)
            m1, m2 = m_ref[hh, 0, rs, :], m_ref[hh, 1, rs, :]
            m = jnp.maximum(jnp.maximum(m0, m1), m2)
            e0, e1, e2 = jnp.exp2(m0 - m), jnp.exp2(m1 - m), jnp.exp2(m2 - m)
            num = e0 * acc0 + e1 * acc_ref[hh, 0, rs, :] + e2 * acc_ref[hh, 1, rs, :]
            den = e0 * l0 + e1 * l_ref[hh, 0, rs, :] + e2 * l_ref[hh, 1, rs, :]
            y_ref[0, rs, out_cols] = (num / den * zg_ref[0, rs, out_cols].astype(F32)).astype(BF16)
            yield

    prev_mix = None
    for hh in range(heads):
        cur = strided_phase(hh)
        if prev_mix is None:
            for _ in cur:
                pass
        else:
            for _ in cur:
                next(prev_mix, None)
                next(cur, None)
            for _ in prev_mix:
                pass
        prev_mix = mix_phase(hh)
    for _ in prev_mix:
        pass


def _dilated(views, zg3, slopes):
    _, bsz, heads, seq, hd = views[0].shape
    assert DILATED_PATTERNS[0][1] == 1 and len(views) == len(DILATED_PATTERNS)
    grp = BAND_HEAD_GROUP
    d4 = DILATED_PATTERNS[1][1]
    assert [p[1] for p in DILATED_PATTERNS] == [1, d4, d4 * d4]
    tables = _band_tables()
    specs = [pl.BlockSpec((3, 1, grp) + v.shape[3:], lambda b, g: (0, b, g, 0, 0)) for v in views]
    tok_spec = pl.BlockSpec((1, seq, grp * hd), lambda b, g: (b, 0, g))
    n_strided = len(DILATED_PATTERNS) - 1
    return pl.pallas_call(
        _dil_kernel,
        grid=(bsz, heads // grp),
        in_specs=[pl.BlockSpec(memory_space=pltpu.SMEM),
                  pl.BlockSpec(tables.shape, lambda b, g: (0, 0, 0))] + specs + [tok_spec],
        out_specs=tok_spec,
        out_shape=jax.ShapeDtypeStruct((bsz, seq, heads * hd), BF16),
        scratch_shapes=([pltpu.VMEM((grp, n_strided, seq, hd), F32) for _ in range(3)]
                        + [pltpu.VMEM((grp, 3, d4, seq // d4, hd), F32)]),
        compiler_params=_params(2),
        name="dilated_mix",
    )(slopes, jnp.asarray(tables), *views, zg3)


def _rope(pe, cc, sa, sb):
    return pe * cc + pltpu.roll(pe, 96, 1) * sa + pltpu.roll(pe, 32, 1) * sb


def _mla_proj_kernel(h_ref, wmid_ref, wbz_ref, wkpe_ref, gq_ref, gkv_ref, wqn_ref, wqp_ref, wkv_ref,
                     cc_ref, sa_ref, sb_ref, zg_ref, bzg_ref, qf_ref, kf_ref, v_ref):
    def proj(w_ref, off, width):
        return _dot_nt(h_ref[...], w_ref[off:off + width, :])

    chunk = PROJ_DOT_COLS
    cc, sa, sb = cc_ref[...], sa_ref[...], sb_ref[...]
    q_scale = (B_NOPE_DIM + B_ROPE_DIM) ** -0.5 * LOG2E
    cqn = _rms(proj(wmid_ref, OFF_CQ - OFF_AZ, Q_LORA_RANK), gq_ref[...]).astype(BF16)
    ckvn = _rms(proj(wmid_ref, OFF_CKV - OFF_AZ, KV_LORA_RANK), gkv_ref[...]).astype(BF16)
    k_rot_low = _rope(proj(wkpe_ref, 0, KPE_PAD), cc, sa, sb)
    for c0 in range(0, A_WIDTH, chunk):
        zg_ref[:, c0:c0 + chunk] = _silu(proj(wmid_ref, c0, chunk)).astype(BF16)
    for c0 in range(0, B_WIDTH, chunk):
        bzg_ref[:, c0:c0 + chunk] = _silu(proj(wbz_ref, c0, chunk)).astype(BF16)
    low_half = lax.broadcasted_iota(jnp.int32, (1, LANES), 1) < B_ROPE_DIM
    q_pe = jnp.dot(cqn, wqp_ref[...], preferred_element_type=F32)
    k_rot = (k_rot_low.astype(BF16), pltpu.roll(k_rot_low, B_ROPE_DIM, 1).astype(BF16))
    kv_w = B_NOPE_DIM + B_V_DIM
    for h0 in range(0, B_HEADS, 2):
        for h in (h0, h0 + 1):
            kv = jnp.dot(ckvn, wkv_ref[:, h * kv_w:(h + 1) * kv_w], preferred_element_type=F32)
            kf_ref[0, h, :, :B_NOPE_DIM] = kv[:, :B_NOPE_DIM].astype(BF16)
            kf_ref[0, h, :, B_NOPE_DIM:] = k_rot[h % 2]
            v_ref[0, h, :, :] = kv[:, B_NOPE_DIM:].astype(BF16)
        q_nope = jnp.dot(cqn, wqn_ref[:, h0 * B_NOPE_DIM:(h0 + 2) * B_NOPE_DIM],
                         preferred_element_type=F32) * q_scale
        q_rot = _rope(q_pe[:, h0 * B_ROPE_DIM:(h0 + 2) * B_ROPE_DIM], cc, sa, sb) * q_scale
        for h in (h0, h0 + 1):
            own = low_half if h == h0 else jnp.logical_not(low_half)
            qf_ref[0, h, :, :B_NOPE_DIM] = (
                q_nope[:, (h - h0) * B_NOPE_DIM:(h - h0 + 1) * B_NOPE_DIM].astype(BF16))
            qf_ref[0, h, :, B_NOPE_DIM:] = jnp.where(own, q_rot, 0.0).astype(BF16)


def _mla_proj(h, w_mid, w_bz, w_kpe, g_q, g_kv, wq_nope, wq_pe, wkv, cc, sa, sb, bsz, seq):
    n_tok, d = h.shape
    tm = ROW_TILE
    per_b = seq // tm
    row = lambda i: (i, 0)
    full = lambda i: (0, 0)
    pos = lambda i: (i % per_b, 0)
    head_major = lambda i: (i // per_b, 0, i % per_b, 0)
    resident = lambda w: pl.BlockSpec(w.shape, full, pipeline_mode=pl.Buffered(1))
    return pl.pallas_call(
        _mla_proj_kernel,
        grid=(n_tok // tm,),
        in_specs=[
            pl.BlockSpec((tm, d), row),
            resident(w_mid), resident(w_bz), resident(w_kpe),
            pl.BlockSpec((1, Q_LORA_RANK), full),
            pl.BlockSpec((1, KV_LORA_RANK), full),
            resident(wq_nope), resident(wq_pe), resident(wkv),
            pl.BlockSpec((tm, LANES), pos),
            pl.BlockSpec((tm, LANES), pos),
            pl.BlockSpec((tm, LANES), pos),
        ],
        out_specs=[
            pl.BlockSpec((tm, A_WIDTH), row),
            pl.BlockSpec((tm, B_WIDTH), row),
            pl.BlockSpec((1, B_HEADS, tm, B_QK_PAD), head_major),
            pl.BlockSpec((1, B_HEADS, tm, B_QK_PAD), head_major),
            pl.BlockSpec((1, B_HEADS, tm, B_V_DIM), head_major),
        ],
        out_shape=[
            jax.ShapeDtypeStruct((n_tok, A_WIDTH), BF16),
            jax.ShapeDtypeStruct((n_tok, B_WIDTH), BF16),
            jax.ShapeDtypeStruct((bsz, B_HEADS, seq, B_QK_PAD), BF16),
            jax.ShapeDtypeStruct((bsz, B_HEADS, seq, B_QK_PAD), BF16),
            jax.ShapeDtypeStruct((bsz, B_HEADS, seq, B_V_DIM), BF16),
        ],
        compiler_params=_params(1),
        name="mla_proj",
    )(h, w_mid, w_bz, w_kpe, g_q.reshape(1, -1), g_kv.reshape(1, -1), wq_nope, wq_pe, wkv,
      cc, sa, sb)


def _mla_attn_kernel(q_ref, k_ref, v_ref, zg_ref, y_ref, vext_ref, s_ref):
    heads = q_ref.shape[1]
    seq = k_ref.shape[2]
    n_blk = seq // MLA_Q
    for hh in range(heads):
        vext_ref[hh, :, :B_V_DIM] = v_ref[0, hh, :, :]
        vext_ref[hh, :, B_V_DIM:] = jnp.ones((seq, B_V_DIM), BF16)

    def scores(hh, blk, slot):
        s_ref[slot] = _dot_nt(q_ref[0, hh, pl.ds(blk * MLA_Q, MLA_Q), :], k_ref[0, hh, :, :])

    def softmax_pv(hh, blk, slot):
        s = s_ref[slot]
        m = jnp.max(s, axis=-1, keepdims=True)
        p = jnp.exp2(s - m).astype(BF16)
        ol = jnp.dot(p, vext_ref[hh], preferred_element_type=F32)
        o = ol[:, :B_V_DIM] / ol[:, B_V_DIM:]
        rows = pl.ds(blk * MLA_Q, MLA_Q)
        cols = slice(hh * B_V_DIM, (hh + 1) * B_V_DIM)
        y_ref[0, rows, cols] = (o * zg_ref[0, rows, cols].astype(F32)).astype(BF16)

    items = [(hh, blk) for hh in range(heads) for blk in range(n_blk)]
    scores(*items[0], 0)
    for n, item in enumerate(items):
        if n + 1 < len(items):
            scores(*items[n + 1], (n + 1) % 2)
        softmax_pv(*item, n % 2)


def _mla_attn(qf, kf, v, bzg3):
    bsz, heads, seq, _ = qf.shape
    grp = MLA_HEAD_GROUP
    head_blk = lambda b, g: (b, g, 0, 0)
    tok_spec = pl.BlockSpec((1, seq, grp * B_V_DIM), lambda b, g: (b, 0, g))
    return pl.pallas_call(
        _mla_attn_kernel,
        grid=(bsz, heads // grp),
        in_specs=[
            pl.BlockSpec((1, grp, seq, B_QK_PAD), head_blk),
            pl.BlockSpec((1, grp, seq, B_QK_PAD), head_blk),
            pl.BlockSpec((1, grp, seq, B_V_DIM), head_blk),
            tok_spec,
        ],
        out_specs=tok_spec,
        out_shape=jax.ShapeDtypeStruct((bsz, seq, heads * B_V_DIM), BF16),
        scratch_shapes=[pltpu.VMEM((grp, seq, 2 * B_V_DIM), BF16), pltpu.VMEM((2, MLA_Q, seq), F32)],
        compiler_params=_params(2),
        name="mla_attn",
    )(qf, kf, v, bzg3)


def _out_kernel(ya_ref, yb_ref, w_ref, x_ref, mod_ref, g_ref, o_ref, y_ref):
    tm, d = y_ref.shape
    chunk = PROJ_DOT_COLS
    ssq = jnp.zeros((tm, LANES), F32)
    for c0 in range(0, d, chunk):
        y = (jnp.dot(ya_ref[...], w_ref[:A_WIDTH, c0:c0 + chunk], preferred_element_type=F32)
             + jnp.dot(yb_ref[...], w_ref[A_WIDTH:, c0:c0 + chunk], preferred_element_type=F32))
        y_ref[:, c0:c0 + chunk] = y
        for t0 in range(0, chunk, LANES):
            ssq = ssq + y[:, t0:t0 + LANES] * y[:, t0:t0 + LANES]
    inv = lax.rsqrt(jnp.sum(ssq, axis=-1, keepdims=True) * (1.0 / d) + NORM_EPS)
    gain = mod_ref[0, 2:3, :] * g_ref[...]
    o_ref[...] = x_ref[...] + y_ref[...] * inv * gain


def _out_proj(ya, yb, w_out, x2, mod3, g_post, seq):
    n_tok, d = x2.shape
    tm = ROW_TILE
    per_b = seq // tm
    row = lambda i: (i, 0)
    return pl.pallas_call(
        _out_kernel,
        grid=(n_tok // tm,),
        in_specs=[
            pl.BlockSpec((tm, A_WIDTH), row),
            pl.BlockSpec((tm, B_WIDTH), row),
            pl.BlockSpec(w_out.shape, lambda i: (0, 0), pipeline_mode=pl.Buffered(1)),
            pl.BlockSpec((tm, d), row),
            pl.BlockSpec((1, 3, d), lambda i: (i // per_b, 0, 0)),
            pl.BlockSpec((1, d), lambda i: (0, 0)),
        ],
        out_specs=pl.BlockSpec((tm, d), row),
        out_shape=jax.ShapeDtypeStruct((n_tok, d), F32),
        scratch_shapes=[pltpu.VMEM((tm, d), F32)],
        compiler_params=_params(1),
        name="out_proj",
    )(ya, yb, w_out, x2, mod3, g_post.reshape(1, d))


def _split_w_uq(w_uq):
    r = w_uq.shape[0]
    w = w_uq.reshape(r, B_HEADS, B_NOPE_DIM + B_ROPE_DIM).astype(BF16)
    return (w[:, :, :B_NOPE_DIM].reshape(r, B_HEADS * B_NOPE_DIM),
            w[:, :, B_NOPE_DIM:].reshape(r, B_HEADS * B_ROPE_DIM))


def _rope_tables(seq):
    half = B_ROPE_DIM // 2
    inv_freq = np.power(ROPE_THETA, -np.arange(half, dtype=np.float64) / half)
    ang = np.arange(seq, dtype=np.float64)[:, None] * inv_freq[None, :]
    cos, sin = np.cos(ang), np.sin(ang)
    z = np.zeros_like(cos)
    cc = np.concatenate([cos, cos, cos, cos], axis=1)
    sa = np.concatenate([-sin, z, -sin, z], axis=1)
    sb = np.concatenate([z, sin, z, sin], axis=1)
    return tuple(jnp.asarray(t.astype(np.float32)) for t in (cc, sa, sb))


def _layer(x, c, w_ada, b_ada, g_pre, w_in, g_q_lora, w_uq, g_kv_lora, w_ukv, w_out, g_post):
    bsz, seq, d = x.shape
    x2 = x.reshape(bsz * seq, d)
    mod3 = _ada(c, w_ada, b_ada).reshape(bsz, 3, d)
    w_qkv, w_mid, w_bz, w_kpe = _cast_w_in(w_in.T)
    h, s1, s4, s16 = _qkv_proj(x2, mod3, g_pre, w_qkv, bsz, seq)
    cc, sa, sb = _rope_tables(seq)
    zg, bzg, qf, kf, v = _mla_proj(h, w_mid, w_bz, w_kpe, g_q_lora, g_kv_lora, *_split_w_uq(w_uq),
                                   w_ukv.astype(BF16), cc, sa, sb, bsz, seq)
    slopes = jnp.asarray(np.exp2(-8.0 * np.arange(1, A_HEADS + 1) / A_HEADS).astype(np.float32))
    ya = _dilated((s1, s4, s16), zg.reshape(bsz, seq, A_WIDTH), slopes)
    yb = _mla_attn(qf, kf, v, bzg.reshape(bsz, seq, B_WIDTH))
    out = _out_proj(ya.reshape(bsz * seq, A_WIDTH), yb.reshape(bsz * seq, B_WIDTH),
                    w_out.astype(BF16), x2, mod3, g_post, seq)
    return out.reshape(bsz, seq, d)


def kernel(x, c, w_ada, b_ada, g_pre, w_in, g_q_lora, w_uq, g_kv_lora, w_ukv, w_out, g_post):
    for layer in range(w_ada.shape[0]):
        x = _layer(x, c, w_ada[layer], b_ada[layer], g_pre[layer], w_in[layer], g_q_lora[layer],
                   w_uq[layer], g_kv_lora[layer], w_ukv[layer], w_out[layer], g_post[layer])
    return x
```
